```python
import jax, jax.numpy as jnp
from jax import lax
import numpy as np

D_MODEL = 2048
BATCH = 32
SEQ = 256
DEPTH = 1
DEC_BATCH = 2
DEC_SEQ = 4096
PAST_LEN = 512

GRID_W = 64
MIX_WIDTH = D_MODEL
LRU_WIDTH = MIX_WIDTH // 2
LRU_HEADS = 8
LRU_HEAD_DIM = LRU_WIDTH // LRU_HEADS
LRU_C = 8.0
CONV_WIDTH = 4
CONV_LEFT = 2
RWKV_WIDTH = MIX_WIDTH - LRU_WIDTH
HEAD_SIZE = 64
RWKV_HEADS = RWKV_WIDTH // HEAD_SIZE
DECAY_LORA = 64
AAA_LORA = 64
GATE_LORA = 160
RWKV_IN_WIDTH = 3 * RWKV_WIDTH + DECAY_LORA + AAA_LORA + GATE_LORA
IN_WIDTH = 2 * LRU_WIDTH + RWKV_IN_WIDTH
FFN_HIDDEN = -(-8 * D_MODEL // (3 * 256)) * 256
RMS_EPS = 1e-6
GN_EPS = 64e-5

kernel_name = 'hybrid_rglru_rwkv7_prefix_diffusion_step'


def rms_norm(x, g):
    x32 = x.astype(jnp.float32)
    y = x32 * lax.rsqrt(jnp.mean(x32 * x32, axis=-1, keepdims=True) + RMS_EPS)
    return (y * g.astype(jnp.float32)).astype(x.dtype)


def shift_context(p):
    half = p.shape[-1] // 2
    prev = jnp.pad(p[..., :half], ((0, 0), (1, 0), (0, 0)))[:, :-1]
    nxt = jnp.pad(p[..., half:], ((0, 0), (0, 1), (0, 0)))[:, 1:]
    return jnp.concatenate([prev, nxt], axis=-1)


def shift_grid(p):
    B, T, C = p.shape
    rows = T // GRID_W
    q = C // 4
    g = p.reshape(B, rows, GRID_W, C)
    left = jnp.pad(g[..., :q], ((0, 0), (0, 0), (1, 0), (0, 0)))[:, :, :-1]
    right = jnp.pad(g[..., q:2 * q], ((0, 0), (0, 0), (0, 1), (0, 0)))[:, :, 1:]
    up = jnp.pad(g[..., 2 * q:3 * q], ((0, 0), (1, 0), (0, 0), (0, 0)))[:, :-1]
    down = jnp.pad(g[..., 3 * q:], ((0, 0), (0, 1), (0, 0), (0, 0)))[:, 1:]
    return jnp.concatenate([left, right, up, down], axis=-1).reshape(B, T, C)


def conv_centred(x, w, b):
    T = x.shape[1]
    xp = jnp.pad(x, ((0, 0), (CONV_LEFT, CONV_WIDTH - 1 - CONV_LEFT), (0, 0)))
    y = xp[:, 0:T] * w[0]
    for k in range(1, CONV_WIDTH):
        y = y + xp[:, k:k + T] * w[k]
    return y + b


def _linear_combine(e1, e2):
    a1, b1 = e1
    a2, b2 = e2
    return a1 * a2, a2 * b1 + b2


def rglru(x, wr, br, wi, bi, lam, h0, reverse):
    B, T, W = x.shape
    xf = x.astype(jnp.float32)
    xh = xf.reshape(B, T, LRU_HEADS, LRU_HEAD_DIM)
    r = jax.nn.sigmoid(jnp.einsum('bthi,hij->bthj', xh, wr.astype(jnp.float32)) + br).reshape(B, T, W)
    i = jax.nn.sigmoid(jnp.einsum('bthi,hij->bthj', xh, wi.astype(jnp.float32)) + bi).reshape(B, T, W)
    log_a = -LRU_C * r * jax.nn.softplus(-lam.astype(jnp.float32))
    a = jnp.exp(log_a)
    b = jnp.sqrt(-jnp.expm1(2.0 * log_a)) * (i * xf)
    if reverse:
        a, b = jnp.flip(a, 1), jnp.flip(b, 1)
    a_cum, b_cum = lax.associative_scan(_linear_combine, (a, b), axis=1)
    hs = a_cum * h0.astype(jnp.float32)[:, None, :] + b_cum
    h_fin = hs[:, -1]
    if reverse:
        hs = jnp.flip(hs, 1)
    return hs, h_fin


def wkv7_scan(r, decay, kk, a, k, v, S0, reverse):
    xs = tuple(jnp.moveaxis(t, 1, 0) for t in (r, decay, kk, a, k, v))

    def step(S, inp):
        r_t, w_t, kk_t, a_t, k_t, v_t = inp
        sa = jnp.einsum('bhvk,bhk->bhv', S, -kk_t)
        S = (S * w_t[:, :, None, :] + sa[..., None] * (kk_t * a_t)[:, :, None, :]
             + v_t[..., None] * k_t[:, :, None, :])
        return S, jnp.einsum('bhvk,bhk->bhv', S, r_t)

    S_fin, ys = lax.scan(step, S0.astype(jnp.float32), xs, reverse=reverse)
    return jnp.moveaxis(ys, 0, 1), S_fin


def rwkv7(pr, shift_fn, S0, L):
    dt = pr.dtype
    B, T, _ = pr.shape
    pr = pr + L['rwkv_mu'] * (shift_fn(pr) - pr)
    s1 = RWKV_WIDTH
    r, k, v, wd, ad, gd = jnp.split(pr, [s1, 2 * s1, 3 * s1, 3 * s1 + DECAY_LORA, 3 * s1 + DECAY_LORA + AAA_LORA], axis=-1)

    def heads(t):
        return t.astype(jnp.float32).reshape(B, T, RWKV_HEADS, HEAD_SIZE)

    kk = heads(k * L['rwkv_k_k'])
    kk = kk * lax.rsqrt(jnp.maximum(jnp.sum(kk * kk, axis=-1, keepdims=True), 1e-24))
    g = jax.nn.sigmoid(gd) @ L['rwkv_g_up']
    wl = jnp.tanh(wd)
    rh, vh = heads(r), heads(v)
    ys, finals = [], []
    for d, rev in ((0, False), (1, True)):
        w_log = -jax.nn.softplus(-(L['rwkv_w0'][d] + wl @ L['rwkv_w_up'][d])) - 0.5
        decay = jnp.exp(-jnp.exp(heads(w_log)))
        a_flat = jax.nn.sigmoid(L['rwkv_a0'][d] + ad @ L['rwkv_a_up'][d])
        k_d = heads(k * (1.0 + (a_flat - 1.0) * L['rwkv_k_a']))
        y_d, S_d = wkv7_scan(rh, decay, kk, heads(a_flat), k_d, vh, S0[:, d], rev)
        ys.append(y_d)
        finals.append(S_d)
    bonus = jnp.sum(rh * heads(k) * L['rwkv_r_k'].astype(jnp.float32), axis=-1, keepdims=True) * vh
    y = ys[0] + ys[1] + bonus
    mean = jnp.mean(y, axis=-1, keepdims=True)
    var = jnp.mean(jnp.square(y - mean), axis=-1, keepdims=True)
    y = ((y - mean) * lax.rsqrt(var + GN_EPS)).reshape(B, T, RWKV_WIDTH)
    y = y * L['rwkv_ln_w'] + L['rwkv_ln_b']
    out = (y * g.astype(jnp.float32)).astype(dt)
    return out, jnp.stack([finals[0], finals[1]], axis=1)


def mixer(h, shift_fn, lru_h0, wkv_S0, L):
    dt = h.dtype
    proj = h @ L['w_in']
    xl, gl, pr = jnp.split(proj, [LRU_WIDTH, 2 * LRU_WIDTH], axis=-1)
    xc = conv_centred(xl, L['lru_conv_w'], L['lru_conv_b'])
    hs_f, hf = rglru(xc, L['lru_wr'][0], L['lru_br'][0], L['lru_wi'][0], L['lru_bi'][0], L['lru_lambda'][0], lru_h0[:, 0], False)
    hs_b, hb = rglru(xc, L['lru_wr'][1], L['lru_br'][1], L['lru_wi'][1], L['lru_bi'][1], L['lru_lambda'][1], lru_h0[:, 1], True)
    lru_out = ((hs_f + hs_b) * jax.nn.gelu(gl.astype(jnp.float32))).astype(dt)
    rwkv_out, wkv_fin = rwkv7(pr, shift_fn, wkv_S0, L)
    out = jnp.concatenate([lru_out, rwkv_out], axis=-1) @ L['w_out']
    return out, jnp.stack([hf, hb], axis=1), wkv_fin


def block(x, mod, shift_fn, lru_h0, wkv_S0, L):
    shift_m, scale_m, gate_m, shift_f, scale_f, gate_f = jnp.split(mod, 6, axis=-1)
    h = rms_norm(x, L['norm_mix_pre']) * (1.0 + scale_m) + shift_m
    out, lru_fin, wkv_fin = mixer(h, shift_fn, lru_h0, wkv_S0, L)
    x = x + gate_m * rms_norm(out, L['norm_mix_post'])
    h = rms_norm(x, L['norm_ffn_pre']) * (1.0 + scale_f) + shift_f
    gate, up = jnp.split(h @ L['ffn_w_gu'], 2, axis=-1)
    f = (jax.nn.silu(gate) * up) @ L['ffn_w_down']
    x = x + gate_f * rms_norm(f, L['norm_ffn_post'])
    return x, lru_fin, wkv_fin


def setup_inputs(seed: int = 0) -> dict:
    key = jax.random.key(seed)
    ks = jax.random.split(key, 40)
    f32 = jnp.float32
    D = D_MODEL

    def nrm(k, shape, scale):
        return jax.random.normal(k, shape, f32) * scale

    u = jax.random.uniform(ks[13], (DEPTH, 2, LRU_WIDTH), f32, minval=0.9, maxval=0.999)
    s = u ** (1.0 / LRU_C)
    lru_lambda = jnp.log(s) - jnp.log1p(-s)
    return {
        'x_prompt': nrm(ks[0], (BATCH, SEQ, D), 1.0),
        'x_sample': nrm(ks[1], (DEC_BATCH, DEC_SEQ, D), 1.0),
        'state_lru': nrm(ks[2], (DEC_BATCH, DEPTH, 2, LRU_WIDTH), 1.0),
        'state_wkv': nrm(ks[3], (DEC_BATCH, DEPTH, 2, RWKV_HEADS, HEAD_SIZE, HEAD_SIZE), 0.5),
        'c': nrm(ks[4], (DEC_BATCH, D), 1.0),
        'c_ctx': nrm(ks[5], (D,), 1.0),
        'norm_mix_pre': 1.0 + nrm(ks[6], (DEPTH, D), 0.1),
        'norm_mix_post': 1.0 + nrm(ks[7], (DEPTH, D), 0.1),
        'norm_ffn_pre': 1.0 + nrm(ks[8], (DEPTH, D), 0.1),
        'norm_ffn_post': 1.0 + nrm(ks[9], (DEPTH, D), 0.1),
        'w_mod': nrm(ks[10], (DEPTH, D, 6 * D), 0.5 * D ** -0.5),
        'b_mod': nrm(ks[11], (DEPTH, 6 * D), 0.02),
        'w_in': nrm(ks[12], (DEPTH, D, IN_WIDTH), D ** -0.5),
        'lru_conv_w': nrm(ks[14], (DEPTH, CONV_WIDTH, LRU_WIDTH), CONV_WIDTH ** -0.5),
        'lru_conv_b': nrm(ks[15], (DEPTH, LRU_WIDTH), 0.02),
        'lru_wr': nrm(ks[16], (DEPTH, 2, LRU_HEADS, LRU_HEAD_DIM, LRU_HEAD_DIM), LRU_HEAD_DIM ** -0.5),
        'lru_br': nrm(ks[17], (DEPTH, 2, LRU_HEADS, LRU_HEAD_DIM), 0.1),
        'lru_wi': nrm(ks[18], (DEPTH, 2, LRU_HEADS, LRU_HEAD_DIM, LRU_HEAD_DIM), LRU_HEAD_DIM ** -0.5),
        'lru_bi': nrm(ks[19], (DEPTH, 2, LRU_HEADS, LRU_HEAD_DIM), 0.1),
        'lru_lambda': lru_lambda,
        'rwkv_mu': jax.random.uniform(ks[20], (DEPTH, RWKV_IN_WIDTH), f32),
        'rwkv_w0': jax.random.uniform(ks[21], (DEPTH, 2, RWKV_WIDTH), f32, minval=-6.0, maxval=-1.0),
        'rwkv_w_up': nrm(ks[22], (DEPTH, 2, DECAY_LORA, RWKV_WIDTH), DECAY_LORA ** -0.5),
        'rwkv_a0': nrm(ks[23], (DEPTH, 2, RWKV_WIDTH), 0.5),
        'rwkv_a_up': nrm(ks[24], (DEPTH, 2, AAA_LORA, RWKV_WIDTH), AAA_LORA ** -0.5),
        'rwkv_g_up': nrm(ks[25], (DEPTH, GATE_LORA, RWKV_WIDTH), GATE_LORA ** -0.5),
        'rwkv_k_k': 0.85 + nrm(ks[26], (DEPTH, RWKV_WIDTH), 0.1),
        'rwkv_k_a': 1.0 + nrm(ks[27], (DEPTH, RWKV_WIDTH), 0.1),
        'rwkv_r_k': nrm(ks[28], (DEPTH, RWKV_HEADS, HEAD_SIZE), 0.1),
        'rwkv_ln_w': 1.0 + nrm(ks[29], (DEPTH, RWKV_WIDTH), 0.1),
        'rwkv_ln_b': nrm(ks[30], (DEPTH, RWKV_WIDTH), 0.02),
        'w_out': nrm(ks[31], (DEPTH, MIX_WIDTH, D), MIX_WIDTH ** -0.5),
        'ffn_w_gu': nrm(ks[32], (DEPTH, D, 2 * FFN_HIDDEN), D ** -0.5),
        'ffn_w_down': nrm(ks[33], (DEPTH, FFN_HIDDEN, D), FFN_HIDDEN ** -0.5),
    }


def reference(x_prompt, x_sample, state_lru, state_wkv, c, c_ctx,
              norm_mix_pre, norm_mix_post, norm_ffn_pre, norm_ffn_post, w_mod, b_mod, w_in,
              lru_conv_w, lru_conv_b, lru_wr, lru_br, lru_wi, lru_bi, lru_lambda,
              rwkv_mu, rwkv_w0, rwkv_w_up, rwkv_a0, rwkv_a_up, rwkv_g_up, rwkv_k_k, rwkv_k_a, rwkv_r_k,
              rwkv_ln_w, rwkv_ln_b, w_out, ffn_w_gu, ffn_w_down):
    y_p = x_prompt
    y_s = x_sample
    Bp = x_prompt.shape[0]
    new_lru, new_wkv = [], []
    for l in range(DEPTH):
        L = {
            'norm_mix_pre': norm_mix_pre[l], 'norm_mix_post': norm_mix_post[l],
            'norm_ffn_pre': norm_ffn_pre[l], 'norm_ffn_post': norm_ffn_post[l],
            'w_in': w_in[l], 'w_out': w_out[l],
            'lru_conv_w': lru_conv_w[l], 'lru_conv_b': lru_conv_b[l],
            'lru_wr': lru_wr[l], 'lru_br': lru_br[l], 'lru_wi': lru_wi[l], 'lru_bi': lru_bi[l],
            'lru_lambda': lru_lambda[l],
            'rwkv_mu': rwkv_mu[l], 'rwkv_w0': rwkv_w0[l], 'rwkv_w_up': rwkv_w_up[l],
            'rwkv_a0': rwkv_a0[l], 'rwkv_a_up': rwkv_a_up[l], 'rwkv_g_up': rwkv_g_up[l],
            'rwkv_k_k': rwkv_k_k[l], 'rwkv_k_a': rwkv_k_a[l], 'rwkv_r_k': rwkv_r_k[l],
            'rwkv_ln_w': rwkv_ln_w[l], 'rwkv_ln_b': rwkv_ln_b[l],
            'ffn_w_gu': ffn_w_gu[l], 'ffn_w_down': ffn_w_down[l],
        }
        mod_ctx = (jax.nn.silu(c_ctx) @ w_mod[l] + b_mod[l])[None, None, :]
        mod_lat = (jax.nn.silu(c) @ w_mod[l] + b_mod[l])[:, None, :]
        lru_zero = jnp.zeros((Bp, 2, LRU_WIDTH), jnp.float32)
        wkv_zero = jnp.zeros((Bp, 2, RWKV_HEADS, HEAD_SIZE, HEAD_SIZE), jnp.float32)
        y_p, lru_ctx, wkv_ctx = block(y_p, mod_ctx, shift_context, lru_zero, wkv_zero, L)
        new_lru.append(lru_ctx.astype(x_prompt.dtype))
        new_wkv.append(wkv_ctx.astype(x_prompt.dtype))
        y_s, _, _ = block(y_s, mod_lat, shift_grid, state_lru[:, l], state_wkv[:, l], L)
    new_state_lru = jnp.stack(new_lru, axis=1)
    new_state_wkv = jnp.stack(new_wkv, axis=1)
    return (y_p, y_s, new_state_lru, new_state_wkv)
```

```python
import functools

import jax
import jax.numpy as jnp
from jax import lax
from jax.experimental import pallas as pl
from jax.experimental.pallas import tpu as pltpu

F32 = jnp.float32
BF16 = jnp.bfloat16

D_MODEL = 2048
LRU_WIDTH = 1024
LRU_HEADS = 8
LRU_HEAD_DIM = 128
LRU_C = 8.0
RWKV_WIDTH = 1024
HEAD_SIZE = 64
RWKV_HEADS = 16
DECAY_LORA = 64
AAA_LORA = 64
GATE_LORA = 160
LORA_WIDTH = DECAY_LORA + AAA_LORA + GATE_LORA
RWKV_IN_WIDTH = 3 * RWKV_WIDTH + LORA_WIDTH
IN_WIDTH = 2 * LRU_WIDTH + RWKV_IN_WIDTH
FFN_HIDDEN = 5632
GRID_W = 64
RMS_EPS = 1e-6
GN_EPS = 64e-5

LANES = 128
PROJ_PAD = 5632
LORA_COL = 2 * LRU_WIDTH + 3 * RWKV_WIDTH
LORA_OUT = 384
CHUNK = 64
PAIRS = RWKV_HEADS // 2
VMEM_LIMIT = 56 * 1024 * 1024


def _cparams(sem):
    return pltpu.CompilerParams(dimension_semantics=sem, vmem_limit_bytes=VMEM_LIMIT)


def _dot(a, b):
    return jnp.dot(a, b, preferred_element_type=F32)


def _dot_nt(a, b):
    return lax.dot_general(a, b, (((1,), (1,)), ((), ())), preferred_element_type=F32)


def _dot_tn(a, b):
    return lax.dot_general(a, b, (((0,), (0,)), ((), ())), preferred_element_type=F32)


def _split3(x):
    hi = x.astype(BF16)
    r1 = x - hi.astype(F32)
    mid = r1.astype(BF16)
    lo = (r1 - mid.astype(F32)).astype(BF16)
    return hi, mid, lo


def _dot_sel_l(sel, x):
    hi, mid, lo = _split3(x)
    return _dot(sel, hi) + _dot(sel, mid) + _dot(sel, lo)


def _dot_sel_r(x, sel):
    hi, mid, lo = _split3(x)
    return _dot(hi, sel) + _dot(mid, sel) + _dot(lo, sel)


def _softplus(x):
    return jnp.maximum(x, 0.0) + jnp.log1p(jnp.exp(-jnp.abs(x)))


def _rms(x, g):
    return x * lax.rsqrt(jnp.mean(x * x, axis=-1, keepdims=True) + RMS_EPS) * g


def _mod_kernel(c_ref, w_ref, b_ref, o_ref):
    c = c_ref[...]
    s = c * jax.nn.sigmoid(c)
    o_ref[...] = jnp.dot(s, w_ref[...], precision=lax.Precision.HIGHEST,
                         preferred_element_type=F32) + b_ref[...]


def _mod(cvec, w_mod, b_mod):
    n = w_mod.shape[1]
    tn = 1024
    return pl.pallas_call(
        _mod_kernel,
        grid=(n // tn,),
        in_specs=[pl.BlockSpec((8, D_MODEL), lambda j: (0, 0)),
                  pl.BlockSpec((D_MODEL, tn), lambda j: (0, j)),
                  pl.BlockSpec((1, tn), lambda j: (0, j))],
        out_specs=pl.BlockSpec((8, tn), lambda j: (0, j)),
        out_shape=jax.ShapeDtypeStruct((8, n), F32),
        compiler_params=_cparams(("parallel",)),
        name="mod",
    )(cvec, w_mod, b_mod)


def _in_proj_kernel(x_ref, mod_ref, g_ref, w_ref, o_ref, h_scr):
    @pl.when(pl.program_id(1) == 0)
    def _():
        m = mod_ref[0]
        h = _rms(x_ref[...], g_ref[...]) * (1.0 + m[1:2]) + m[0:1]
        h_scr[...] = h.astype(BF16)

    o_ref[...] = _dot(h_scr[...], w_ref[...])


def _in_proj(x2d, mod3, g, w_in_p, mod_idx, tm=512, tn=512):
    m = x2d.shape[0]
    return pl.pallas_call(
        _in_proj_kernel,
        grid=(m // tm, PROJ_PAD // tn),
        in_specs=[pl.BlockSpec((tm, D_MODEL), lambda i, j: (i, 0)),
                  pl.BlockSpec((1, 6, D_MODEL), lambda i, j: (mod_idx(i * tm), 0, 0)),
                  pl.BlockSpec((1, D_MODEL), lambda i, j: (0, 0)),
                  pl.BlockSpec((D_MODEL, tn), lambda i, j: (0, j))],
        out_specs=pl.BlockSpec((tm, tn), lambda i, j: (i, j)),
        out_shape=jax.ShapeDtypeStruct((m, PROJ_PAD), F32),
        scratch_shapes=[pltpu.VMEM((tm, D_MODEL), BF16)],
        compiler_params=_cparams(("parallel", "arbitrary")),
        name="in_proj",
    )(x2d, mod3, g, w_in_p)


def _lru_kernel(xl_ref, gl_ref, cw_ref, cb_ref, wg_ref, bg_ref, lam_ref, h0_ref,
                out_ref, hfin_ref, xpad, a_f, b_f, a_b, b_b, *, seq, tile):
    n_tiles = seq // tile
    zeros8 = jnp.zeros((8, LANES), F32)
    xpad[pl.ds(0, 8), :] = zeros8
    xpad[pl.ds(seq + 8, 8), :] = zeros8

    def copy_body(i, c):
        r0 = pl.multiple_of(i * tile, tile)
        xpad[pl.ds(r0 + 8, tile), :] = xl_ref[0, pl.ds(r0, tile), :]
        return c

    lax.fori_loop(0, n_tiles, copy_body, 0)

    cneg = -LRU_C * _softplus(-lam_ref[0])
    cw = cw_ref[...]
    cb = cb_ref[...]
    bg = bg_ref[0]

    def gate_body(i, c):
        r0 = pl.multiple_of(i * tile, tile)
        ext = xpad[pl.ds(r0, tile + 16), :]
        n_ext = tile + 16
        xc = (pltpu.roll(ext, 2, 0)[8:8 + tile] * cw[0:1]
              + pltpu.roll(ext, 1, 0)[8:8 + tile] * cw[1:2]
              + ext[8:8 + tile] * cw[2:3]
              + pltpu.roll(ext, n_ext - 1, 0)[8:8 + tile] * cw[3:4]) + cb
        g = _dot(xc.astype(BF16), wg_ref[0]) + bg
        for d, (a_s, b_s) in enumerate(((a_f, b_f), (a_b, b_b))):
            r = jax.nn.sigmoid(g[:, 256 * d:256 * d + 128])
            ig = jax.nn.sigmoid(g[:, 256 * d + 128:256 * d + 256])
            log_a = cneg[d:d + 1] * r
            a = jnp.exp(log_a)
            one_m_a2 = -jnp.tanh(log_a) * (a * a + 1.0)
            a_s[pl.ds(r0, tile), :] = a
            b_s[pl.ds(r0, tile), :] = jnp.sqrt(one_m_a2) * (ig * xc)
        return c

    lax.fori_loop(0, n_tiles, gate_body, 0)

    n8 = seq // 8

    def scan_body(i, carry):
        hf, hb = carry
        r0 = pl.multiple_of(i * 8, 8)
        a = a_f[pl.ds(r0, 8), :]
        b = b_f[pl.ds(r0, 8), :]
        rows = []
        for r in range(8):
            hf = a[r:r + 1] * hf + b[r:r + 1]
            rows.append(hf)
        b_f[pl.ds(r0, 8), :] = jnp.concatenate(rows, axis=0)
        r1 = pl.multiple_of((n8 - 1 - i) * 8, 8)
        a = a_b[pl.ds(r1, 8), :]
        b = b_b[pl.ds(r1, 8), :]
        rows = [None] * 8
        for r in range(7, -1, -1):
            hb = a[r:r + 1] * hb + b[r:r + 1]
            rows[r] = hb
        b_b[pl.ds(r1, 8), :] = jnp.concatenate(rows, axis=0)
        return hf, hb

    h0 = h0_ref[0]
    hf, hb = lax.fori_loop(0, n8, scan_body, (h0[0:1], h0[1:2]))
    hfin_ref[0] = jnp.concatenate([hf, hb], axis=0)

    def out_body(i, c):
        r0 = pl.multiple_of(i * tile, tile)
        gl = gl_ref[0, pl.ds(r0, tile), :]
        hs = b_f[pl.ds(r0, tile), :] + b_b[pl.ds(r0, tile), :]
        out_ref[0, pl.ds(r0, tile), :] = (hs * jax.nn.gelu(gl)).astype(BF16)
        return c

    lax.fori_loop(0, n_tiles, out_body, 0)


def _lru(proj3, conv_w, conv_b, wg, bg, lam, h0):
    nb, seq, _ = proj3.shape
    tile = 256
    kern = functools.partial(_lru_kernel, seq=seq, tile=tile)
    return pl.pallas_call(
        kern,
        grid=(nb, LRU_HEADS),
        in_specs=[pl.BlockSpec((1, seq, LANES), lambda b, h: (b, 0, h)),
                  pl.BlockSpec((1, seq, LANES), lambda b, h: (b, 0, LRU_HEADS + h)),
                  pl.BlockSpec((4, LANES), lambda b, h: (0, h)),
                  pl.BlockSpec((1, LANES), lambda b, h: (0, h)),
                  pl.BlockSpec((1, LANES, 512), lambda b, h: (h, 0, 0)),
                  pl.BlockSpec((1, 1, 512), lambda b, h: (h, 0, 0)),
                  pl.BlockSpec((1, 2, LANES), lambda b, h: (h, 0, 0)),
                  pl.BlockSpec((1, 2, LANES), lambda b, h: (b, 0, h))],
        out_specs=[pl.BlockSpec((1, seq, LANES), lambda b, h: (b, 0, h)),
                   pl.BlockSpec((1, 2, LANES), lambda b, h: (b, 0, h))],
        out_shape=[jax.ShapeDtypeStruct((nb, seq, LRU_WIDTH), BF16),
                   jax.ShapeDtypeStruct((nb, 2, LRU_WIDTH), F32)],
        scratch_shapes=[pltpu.VMEM((seq + 16, LANES), F32)] + [pltpu.VMEM((seq, LANES), F32)] * 4,
        compiler_params=_cparams(("parallel", "arbitrary")),
        name="lru",
    )(proj3, proj3, conv_w, conv_b, wg, bg, lam, h0)


def _shift_select(x_ref, xpad, r0, rows, seq, cg, grid_mode):
    cur = x_ref[0, pl.ds(r0, rows), :]
    if grid_mode:
        rid = lax.broadcasted_iota(jnp.int32, cur.shape, 0)
        left = jnp.where(rid == 0, 0.0, pltpu.roll(cur, 1, 0))
        right = jnp.where(rid == rows - 1, 0.0, pltpu.roll(cur, rows - 1, 0))
        up_r = pl.multiple_of(jnp.maximum(r0 - GRID_W, 0), GRID_W)
        dn_r = pl.multiple_of(jnp.minimum(r0 + GRID_W, seq - GRID_W), GRID_W)
        up = jnp.where(r0 > 0, x_ref[0, pl.ds(up_r, rows), :], 0.0)
        down = jnp.where(r0 < seq - GRID_W, x_ref[0, pl.ds(dn_r, rows), :], 0.0)
        q = RWKV_IN_WIDTH // 4
        sh = jnp.where(cg < q, left, jnp.where(cg < 2 * q, right, jnp.where(cg < 3 * q, up, down)))
    else:
        ext = xpad[pl.ds(r0, rows + 16), :]
        prev = pltpu.roll(ext, 1, 0)[8:8 + rows]
        nxt = pltpu.roll(ext, rows + 15, 0)[8:8 + rows]
        sh = jnp.where(cg < RWKV_IN_WIDTH // 2, prev, nxt)
    return cur, sh


def _fill_xpad(x_ref, xpad, seq, tile):
    width = xpad.shape[1]
    zeros8 = jnp.zeros((8, width), F32)
    xpad[pl.ds(0, 8), :] = zeros8
    xpad[pl.ds(seq + 8, 8), :] = zeros8

    def copy_body(i, c):
        r0 = pl.multiple_of(i * tile, tile)
        xpad[pl.ds(r0 + 8, tile), :] = x_ref[0, pl.ds(r0, tile), :]
        return c

    lax.fori_loop(0, seq // tile, copy_body, 0)


def _shift_main_kernel(x_ref, mu_ref, o_ref, xpad, *, seq, grid_mode):
    rows = GRID_W if grid_mode else 256
    if not grid_mode:
        _fill_xpad(x_ref, xpad, seq, rows)
    mu = mu_ref[...]
    col0 = pl.program_id(1) * LANES

    def body(i, c):
        r0 = pl.multiple_of(i * rows, rows)
        cg = col0 + lax.broadcasted_iota(jnp.int32, (rows, LANES), 1)
        cur, sh = _shift_select(x_ref, xpad, r0, rows, seq, cg, grid_mode)
        o_ref[0, pl.ds(r0, rows), :] = cur + mu * (sh - cur)
        return c

    lax.fori_loop(0, seq // rows, body, 0)


def _shift_main(proj3, mu_main, grid_mode):
    nb, seq, _ = proj3.shape
    nblk = 3 * RWKV_WIDTH // LANES
    first = 2 * LRU_WIDTH // LANES
    kern = functools.partial(_shift_main_kernel, seq=seq, grid_mode=grid_mode)
    return pl.pallas_call(
        kern,
        grid=(nb, nblk),
        in_specs=[pl.BlockSpec((1, seq, LANES), lambda b, j: (b, 0, first + j)),
                  pl.BlockSpec((1, LANES), lambda b, j: (0, j))],
        out_specs=pl.BlockSpec((1, seq, LANES), lambda b, j: (b, 0, j)),
        out_shape=jax.ShapeDtypeStruct((nb, seq, 3 * RWKV_WIDTH), F32),
        scratch_shapes=[pltpu.VMEM((8, LANES) if grid_mode else (seq + 16, LANES), F32)],
        compiler_params=_cparams(("parallel", "parallel")),
        name="shift_main",
    )(proj3, mu_main)


def _shift_lora_kernel(x_ref, mu_ref, o_ref, xpad, *, seq, grid_mode):
    rows = GRID_W if grid_mode else 256
    if not grid_mode:
        _fill_xpad(x_ref, xpad, seq, rows)
    mu = mu_ref[...]

    def body(i, c):
        r0 = pl.multiple_of(i * rows, rows)
        cg = 3 * RWKV_WIDTH + lax.broadcasted_iota(jnp.int32, (rows, 512), 1)
        cur, sh = _shift_select(x_ref, xpad, r0, rows, seq, cg, grid_mode)
        y = (cur + mu * (sh - cur))[:, :LORA_OUT]
        lane = lax.broadcasted_iota(jnp.int32, y.shape, 1)
        act = jnp.where(lane < DECAY_LORA, jnp.tanh(y),
                        jnp.where(lane < DECAY_LORA + AAA_LORA, y, jax.nn.sigmoid(y)))
        o_ref[0, pl.ds(r0, rows), :] = act.astype(BF16)
        return c

    lax.fori_loop(0, seq // rows, body, 0)


def _shift_lora(proj3, mu_lora, grid_mode):
    nb, seq, _ = proj3.shape
    kern = functools.partial(_shift_lora_kernel, seq=seq, grid_mode=grid_mode)
    return pl.pallas_call(
        kern,
        grid=(nb,),
        in_specs=[pl.BlockSpec((1, seq, 512), lambda b: (b, 0, LORA_COL // 512)),
                  pl.BlockSpec((1, 512), lambda b: (0, 0))],
        out_specs=pl.BlockSpec((1, seq, LORA_OUT), lambda b: (b, 0, 0)),
        out_shape=jax.ShapeDtypeStruct((nb, seq, LORA_OUT), BF16),
        scratch_shapes=[pltpu.VMEM((8, 512) if grid_mode else (seq + 16, 512), F32)],
        compiler_params=_cparams(("parallel",)),
        name="shift_lora",
    )(proj3, mu_lora)


_V_W0, _V_A0, _V_KK, _V_KA, _V_RK, _V_LNW, _V_LNB = 0, 2, 4, 5, 6, 7, 8


def _wkv_kernel(r_ref, k_ref, v_ref, lo_ref, ww_ref, wa_ref, wgt_ref, vec_ref, s0_ref,
                out_ref, sfin_ref,
                t_s, a_s, ar_s, bk_s, v_s, xv_s, yv_s, gam_s, ybuf, s_scr,
                *, seq, sc):
    n_chunks = seq // CHUNK
    n_super = n_chunks // sc
    C = CHUNK
    vec = vec_ref[0]
    lane = lax.broadcasted_iota(jnp.int32, (C, LANES), 1)
    head0 = lane < HEAD_SIZE
    ri = lax.broadcasted_iota(jnp.int32, (LANES, LANES), 0)
    ci = lax.broadcasted_iota(jnp.int32, (LANES, LANES), 1)
    seg_ones = ((ri // HEAD_SIZE) == (ci // HEAD_SIZE)).astype(BF16)
    eye = (ri == ci).astype(F32)
    tr = lax.broadcasted_iota(jnp.int32, (C, C), 0)
    tc = lax.broadcasted_iota(jnp.int32, (C, C), 1)
    tri = ((tc <= tr).astype(BF16), (tc >= tr).astype(BF16))
    br = lax.broadcasted_iota(jnp.int32, (2 * LANES, 2 * LANES), 0)
    bc = lax.broadcasted_iota(jnp.int32, (2 * LANES, 2 * LANES), 1)
    brt, bct = br % C, bc % C
    incl = (br >= LANES).astype(jnp.int32)
    big_mask = (bct < brt + incl, bct > brt - incl)

    def stack(x):
        return jnp.concatenate([jnp.where(head0, x, 0.0), jnp.where(head0, 0.0, x)], axis=0).astype(BF16)

    def prep(chunk, d, slot):
        r0 = pl.multiple_of(chunk * C, C)
        r = r_ref[0, pl.ds(r0, C), :]
        k = k_ref[0, pl.ds(r0, C), :]
        v = v_ref[0, pl.ds(r0, C), :]
        lo = lo_ref[0, pl.ds(r0, C), :]
        lw = _dot(lo[:, 0:LANES], ww_ref[0])[:, d * LANES:(d + 1) * LANES]
        la = _dot(lo[:, 0:LANES], wa_ref[0])[:, d * LANES:(d + 1) * LANES]
        w_log = -_softplus(-(vec[_V_W0 + d:_V_W0 + d + 1] + lw)) - 0.5
        logw = -jnp.exp(w_log)
        a = jax.nn.sigmoid(vec[_V_A0 + d:_V_A0 + d + 1] + la)
        kk = k * vec[_V_KK:_V_KK + 1]
        kk = kk * lax.rsqrt(jnp.maximum(_dot_sel_r(kk * kk, seg_ones), 1e-24))
        kd = k * (1.0 + (a - 1.0) * vec[_V_KA:_V_KA + 1])
        bt = kk * a
        cl = _dot_sel_l(tri[d], logw)
        ltot = jnp.sum(logw, axis=0, keepdims=True)
        e_neg = jnp.exp(-cl)
        e_tot = jnp.exp(ltot - cl)
        ar = jnp.concatenate([stack(-kk * jnp.exp(cl - logw)), stack(r * jnp.exp(cl))], axis=0)
        bk = jnp.concatenate([stack(bt * e_neg), stack(kd * e_neg)], axis=0)
        vs = stack(v)
        big = jnp.where(big_mask[d], _dot_nt(ar, bk), 0.0)
        n = big[0:LANES, 0:LANES]
        p = eye + n
        nb = n.astype(BF16)
        for _ in range(5):
            nb = _dot(nb, nb).astype(BF16)
            p = p + _dot(p.astype(BF16), nb)
        xyv = _dot(big[:, LANES:].astype(BF16), vs)
        t_s[slot] = p.astype(BF16)
        a_s[slot] = big[LANES:, 0:LANES].astype(BF16)
        ar_s[slot] = ar
        bk_s[slot] = jnp.concatenate([stack(bt * e_tot), stack(kd * e_tot)], axis=0)
        v_s[slot] = vs
        xv_s[slot] = xyv[0:LANES]
        yv_s[slot] = xyv[LANES:]
        gam_s[slot] = jnp.broadcast_to(jnp.exp(ltot), (8, LANES))

    def step(chunk, slot, s):
        ars = _dot_nt(ar_s[slot], s.astype(BF16))
        x = ars[0:LANES] + xv_s[slot]
        u = _dot(t_s[slot], x.astype(BF16))
        ub = u.astype(BF16)
        ysm = ars[LANES:] + yv_s[slot] + _dot(a_s[slot], ub)
        r0 = pl.multiple_of(chunk * C, C)
        ybuf[pl.ds(r0, C), :] += ysm[0:C] + ysm[C:]
        uv = jnp.concatenate([ub, v_s[slot]], axis=0)
        return s * gam_s[slot][0:1] + _dot_tn(uv, bk_s[slot])

    def zero_body(i, c):
        ybuf[pl.ds(pl.multiple_of(i * C, C), C), :] = jnp.zeros((C, LANES), F32)
        return c

    lax.fori_loop(0, n_chunks, zero_body, 0)
    s_scr[0] = s0_ref[0, 0]
    s_scr[1] = s0_ref[0, 1]

    def super_body(sci, c):
        base_f = sci * sc
        base_b = (n_super - 1 - sci) * sc

        def prep_body(lc, c2):
            prep(base_f + lc, 0, lc)
            prep(base_b + lc, 1, sc + lc)
            return c2

        lax.fori_loop(0, sc, prep_body, 0)

        def step_body(lc, carry):
            sf, sb = carry
            sf = step(base_f + lc, lc, sf)
            lb = sc - 1 - lc
            sb = step(base_b + lb, sc + lb, sb)
            return sf, sb

        sf, sb = lax.fori_loop(0, sc, step_body, (s_scr[0], s_scr[1]))
        s_scr[0] = sf
        s_scr[1] = sb
        return c

    lax.fori_loop(0, n_super, super_body, 0)
    sfin_ref[0, 0] = s_scr[0]
    sfin_ref[0, 1] = s_scr[1]

    def out_body(i, c):
        r0 = pl.multiple_of(i * C, C)
        r = r_ref[0, pl.ds(r0, C), :]
        k = k_ref[0, pl.ds(r0, C), :]
        v = v_ref[0, pl.ds(r0, C), :]
        lo = lo_ref[0, pl.ds(r0, C), :]
        bonus = _dot_sel_r(r * k * vec[_V_RK:_V_RK + 1], seg_ones) * v
        y = ybuf[pl.ds(r0, C), :] + bonus
        mean = _dot_sel_r(y, seg_ones) * (1.0 / HEAD_SIZE)
        dlt = y - mean
        var = _dot_sel_r(dlt * dlt, seg_ones) * (1.0 / HEAD_SIZE)
        yn = dlt * lax.rsqrt(var + GN_EPS) * vec[_V_LNW:_V_LNW + 1] + vec[_V_LNB:_V_LNB + 1]
        g = _dot(lo[:, LANES:LORA_OUT], wgt_ref[0])
        out_ref[0, pl.ds(r0, C), :] = (yn * g).astype(BF16)
        return c

    lax.fori_loop(0, n_chunks, out_body, 0)


def _wkv(prs, lora, ww, wa, wgt, vec, s0):
    nb, seq, _ = prs.shape
    n_chunks = seq // CHUNK
    sc = min(8, n_chunks)
    kern = functools.partial(_wkv_kernel, seq=seq, sc=sc)
    nblk = RWKV_WIDTH // LANES
    slots = 2 * sc
    return pl.pallas_call(
        kern,
        grid=(nb, PAIRS),
        in_specs=[pl.BlockSpec((1, seq, LANES), lambda b, j: (b, 0, j)),
                  pl.BlockSpec((1, seq, LANES), lambda b, j: (b, 0, nblk + j)),
                  pl.BlockSpec((1, seq, LANES), lambda b, j: (b, 0, 2 * nblk + j)),
                  pl.BlockSpec((1, seq, LORA_OUT), lambda b, j: (b, 0, 0)),
                  pl.BlockSpec((1, LANES, 2 * LANES), lambda b, j: (j, 0, 0)),
                  pl.BlockSpec((1, LANES, 2 * LANES), lambda b, j: (j, 0, 0)),
                  pl.BlockSpec((1, 2 * LANES, LANES), lambda b, j: (j, 0, 0)),
                  pl.BlockSpec((1, 16, LANES), lambda b, j: (j, 0, 0)),
                  pl.BlockSpec((1, 2, LANES, LANES), lambda b, j: (b * PAIRS + j, 0, 0, 0))],
        out_specs=[pl.BlockSpec((1, seq, LANES), lambda b, j: (b, 0, j)),
                   pl.BlockSpec((1, 2, LANES, LANES), lambda b, j: (b * PAIRS + j, 0, 0, 0))],
        out_shape=[jax.ShapeDtypeStruct((nb, seq, RWKV_WIDTH), BF16),
                   jax.ShapeDtypeStruct((nb * PAIRS, 2, LANES, LANES), F32)],
        scratch_shapes=[pltpu.VMEM((slots, LANES, LANES), BF16),
                        pltpu.VMEM((slots, LANES, LANES), BF16),
                        pltpu.VMEM((slots, 2 * LANES, LANES), BF16),
                        pltpu.VMEM((slots, 2 * LANES, LANES), BF16),
                        pltpu.VMEM((slots, LANES, LANES), BF16),
                        pltpu.VMEM((slots, LANES, LANES), F32),
                        pltpu.VMEM((slots, LANES, LANES), F32),
                        pltpu.VMEM((slots, 8, LANES), F32),
                        pltpu.VMEM((seq, LANES), F32),
                        pltpu.VMEM((2, LANES, LANES), F32)],
        compiler_params=_cparams(("parallel", "arbitrary")),
        name="wkv",
    )(prs, prs, prs, lora, ww, wa, wgt, vec, s0)


def _out_proj_kernel(lru_ref, rw_ref, w1_ref, w2_ref, x_ref, mod_ref, g_ref, x1_ref, h2_ref):
    out = _dot(lru_ref[...], w1_ref[...]) + _dot(rw_ref[...], w2_ref[...])
    m = mod_ref[0]
    g = g_ref[...]
    x1 = x_ref[...] + m[2:3] * _rms(out, g[0:1])
    x1_ref[...] = x1
    h2_ref[...] = (_rms(x1, g[1:2]) * (1.0 + m[4:5]) + m[3:4]).astype(BF16)


def _out_proj(lru_o, rw_o, w_out_b, x2d, mod3, g2, mod_idx, tm=512):
    m = x2d.shape[0]
    half = D_MODEL // 2
    return pl.pallas_call(
        _out_proj_kernel,
        grid=(m // tm,),
        in_specs=[pl.BlockSpec((tm, half), lambda i: (i, 0)),
                  pl.BlockSpec((tm, half), lambda i: (i, 0)),
                  pl.BlockSpec((half, D_MODEL), lambda i: (0, 0)),
                  pl.BlockSpec((half, D_MODEL), lambda i: (1, 0)),
                  pl.BlockSpec((tm, D_MODEL), lambda i: (i, 0)),
                  pl.BlockSpec((1, 6, D_MODEL), lambda i: (mod_idx(i * tm), 0, 0)),
                  pl.BlockSpec((2, D_MODEL), lambda i: (0, 0))],
        out_specs=[pl.BlockSpec((tm, D_MODEL), lambda i: (i, 0)),
                   pl.BlockSpec((tm, D_MODEL), lambda i: (i, 0))],
        out_shape=[jax.ShapeDtypeStruct((m, D_MODEL), F32),
                   jax.ShapeDtypeStruct((m, D_MODEL), BF16)],
        compiler_params=_cparams(("parallel",)),
        name="out_proj",
    )(lru_o, rw_o, w_out_b, w_out_b, x2d, mod3, g2)


def _ffn_kernel(h_ref, wg_ref, wu_ref, wd_ref, x1_ref, mod_ref, g_ref, o_ref, acc):
    f = pl.program_id(1)
    h = h_ref[...]
    gate = _dot(h, wg_ref[...])
    up = _dot(h, wu_ref[...])
    act = (gate * jax.nn.sigmoid(gate) * up).astype(BF16)
    part = _dot(act, wd_ref[...])

    @pl.when(f == 0)
    def _():
        acc[...] = part

    @pl.when(f > 0)
    def _():
        acc[...] += part

    @pl.when(f == pl.num_programs(1) - 1)
    def _():
        o_ref[...] = x1_ref[...] + mod_ref[0][5:6] * _rms(acc[...], g_ref[...])


def _ffn(h2, w_gu_b, w_down_b, x1, mod3, g, mod_idx, tm=512, tf=512):
    m = h2.shape[0]
    nf = FFN_HIDDEN // tf
    return pl.pallas_call(
        _ffn_kernel,
        grid=(m // tm, nf),
        in_specs=[pl.BlockSpec((tm, D_MODEL), lambda i, f: (i, 0)),
                  pl.BlockSpec((D_MODEL, tf), lambda i, f: (0, f)),
                  pl.BlockSpec((D_MODEL, tf), lambda i, f: (0, nf + f)),
                  pl.BlockSpec((tf, D_MODEL), lambda i, f: (f, 0)),
                  pl.BlockSpec((tm, D_MODEL), lambda i, f: (i, 0)),
                  pl.BlockSpec((1, 6, D_MODEL), lambda i, f: (mod_idx(i * tm), 0, 0)),
                  pl.BlockSpec((1, D_MODEL), lambda i, f: (0, 0))],
        out_specs=pl.BlockSpec((tm, D_MODEL), lambda i, f: (i, 0)),
        out_shape=jax.ShapeDtypeStruct((m, D_MODEL), F32),
        scratch_shapes=[pltpu.VMEM((tm, D_MODEL), F32)],
        compiler_params=_cparams(("parallel", "arbitrary")),
        name="ffn",
    )(h2, w_gu_b, w_gu_b, w_down_b, x1, mod3, g)


def _block_diag_pairs(s):
    nb = s.shape[0]
    s = s.reshape(nb, 2, PAIRS, 2, HEAD_SIZE, HEAD_SIZE)
    z = jnp.zeros((nb, 2, PAIRS, 2, HEAD_SIZE, 2, HEAD_SIZE), F32)
    z = z.at[:, :, :, 0, :, 0, :].set(s[:, :, :, 0]).at[:, :, :, 1, :, 1, :].set(s[:, :, :, 1])
    z = z.reshape(nb, 2, PAIRS, LANES, LANES)
    return jnp.swapaxes(z, 1, 2).reshape(nb * PAIRS, 2, LANES, LANES)


def _pair_diag_blocks(sbd, nb):
    z = sbd.reshape(nb, PAIRS, 2, 2, HEAD_SIZE, 2, HEAD_SIZE)
    heads = jnp.stack([z[:, :, :, 0, :, 0, :], z[:, :, :, 1, :, 1, :]], axis=3)
    return jnp.swapaxes(heads, 1, 2).reshape(nb, 2, RWKV_HEADS, HEAD_SIZE, HEAD_SIZE)


def _group(x3, mod3, mod_idx, grid_mode, lru_h0, wkv_s0, P):
    nb, seq, _ = x3.shape
    x2d = x3.reshape(nb * seq, D_MODEL)
    proj = _in_proj(x2d, mod3, P["norm_mix_pre"], P["w_in"], mod_idx).reshape(nb, seq, PROJ_PAD)
    lru_o, lru_fin = _lru(proj, P["conv_w"], P["conv_b"], P["lru_wg"], P["lru_bg"], P["lru_lam"], lru_h0)
    prs = _shift_main(proj, P["mu_main"], grid_mode)
    lora = _shift_lora(proj, P["mu_lora"], grid_mode)
    rw_o, s_fin = _wkv(prs, lora, P["ww"], P["wa"], P["wgt"], P["vec"], wkv_s0)
    x1, h2 = _out_proj(lru_o.reshape(nb * seq, LRU_WIDTH), rw_o.reshape(nb * seq, RWKV_WIDTH),
                       P["w_out"], x2d, mod3, P["g_post_pre"], mod_idx)
    y = _ffn(h2, P["w_gu"], P["w_down"], x1, mod3, P["norm_ffn_post"], mod_idx)
    return y.reshape(nb, seq, D_MODEL), lru_fin, s_fin


def _prep_params(norm_mix_pre, norm_mix_post, norm_ffn_pre, norm_ffn_post, w_in,
                 lru_conv_w, lru_conv_b, lru_wr, lru_br, lru_wi, lru_bi, lru_lambda,
                 rwkv_mu, rwkv_w0, rwkv_w_up, rwkv_a0, rwkv_a_up, rwkv_g_up, rwkv_k_k, rwkv_k_a, rwkv_r_k,
                 rwkv_ln_w, rwkv_ln_b, w_out, ffn_w_gu, ffn_w_down):
    P = {}
    P["norm_mix_pre"] = norm_mix_pre.reshape(1, D_MODEL)
    P["g_post_pre"] = jnp.stack([norm_mix_post, norm_ffn_pre], axis=0)
    P["norm_ffn_post"] = norm_ffn_post.reshape(1, D_MODEL)
    P["w_in"] = jnp.pad(w_in.astype(BF16), ((0, 0), (0, PROJ_PAD - IN_WIDTH)))
    P["conv_w"] = lru_conv_w
    P["conv_b"] = lru_conv_b.reshape(1, LRU_WIDTH)
    P["lru_wg"] = jnp.concatenate([lru_wr[0], lru_wi[0], lru_wr[1], lru_wi[1]], axis=-1).astype(BF16)
    P["lru_bg"] = jnp.concatenate([lru_br[0], lru_bi[0], lru_br[1], lru_bi[1]], axis=-1).reshape(LRU_HEADS, 1, 512)
    P["lru_lam"] = jnp.swapaxes(lru_lambda.reshape(2, LRU_HEADS, LRU_HEAD_DIM), 0, 1)
    P["mu_main"] = rwkv_mu[:3 * RWKV_WIDTH].reshape(1, 3 * RWKV_WIDTH)
    P["mu_lora"] = jnp.pad(rwkv_mu[3 * RWKV_WIDTH:], (0, 512 - LORA_WIDTH)).reshape(1, 512)

    def per_pair(w):
        return jnp.swapaxes(w.reshape(w.shape[0], PAIRS, LANES), 0, 1)

    wu = jnp.concatenate([per_pair(rwkv_w_up[0]), per_pair(rwkv_w_up[1])], axis=-1)
    au = jnp.concatenate([per_pair(rwkv_a_up[0]), per_pair(rwkv_a_up[1])], axis=-1)
    P["ww"] = jnp.pad(wu, ((0, 0), (0, LANES - DECAY_LORA), (0, 0))).astype(BF16)
    P["wa"] = jnp.pad(au, ((0, 0), (DECAY_LORA, LANES - DECAY_LORA - AAA_LORA), (0, 0))).astype(BF16)
    P["wgt"] = jnp.pad(per_pair(rwkv_g_up), ((0, 0), (0, 2 * LANES - GATE_LORA), (0, 0))).astype(BF16)
    rows = [rwkv_w0[0], rwkv_w0[1], rwkv_a0[0], rwkv_a0[1], rwkv_k_k, rwkv_k_a,
            rwkv_r_k.reshape(RWKV_WIDTH), rwkv_ln_w, rwkv_ln_b]
    vec = jnp.stack(rows + [jnp.zeros_like(rwkv_k_k)] * (16 - len(rows)), axis=0)
    P["vec"] = per_pair(vec)
    P["w_out"] = w_out.astype(BF16)
    P["w_gu"] = ffn_w_gu.astype(BF16)
    P["w_down"] = ffn_w_down.astype(BF16)
    return P


def kernel(x_prompt, x_sample, state_lru, state_wkv, c, c_ctx, norm_mix_pre, norm_mix_post, norm_ffn_pre,
           norm_ffn_post, w_mod, b_mod, w_in, lru_conv_w, lru_conv_b, lru_wr, lru_br, lru_wi, lru_bi,
           lru_lambda, rwkv_mu, rwkv_w0, rwkv_w_up, rwkv_a0, rwkv_a_up, rwkv_g_up, rwkv_k_k, rwkv_k_a,
           rwkv_r_k, rwkv_ln_w, rwkv_ln_b, w_out, ffn_w_gu, ffn_w_down):
    depth = w_in.shape[0]
    nb_p = x_prompt.shape[0]
    nb_s, seq_s, _ = x_sample.shape
    y_p, y_s = x_prompt, x_sample
    new_lru, new_wkv = [], []
    cvec = jnp.concatenate([c_ctx[None, :], c, jnp.zeros((8 - 1 - nb_s, D_MODEL), F32)], axis=0)
    for l in range(depth):
        P = _prep_params(norm_mix_pre[l], norm_mix_post[l], norm_ffn_pre[l], norm_ffn_post[l], w_in[l],
                         lru_conv_w[l], lru_conv_b[l], lru_wr[l], lru_br[l], lru_wi[l], lru_bi[l],
                         lru_lambda[l], rwkv_mu[l], rwkv_w0[l], rwkv_w_up[l], rwkv_a0[l], rwkv_a_up[l],
                         rwkv_g_up[l], rwkv_k_k[l], rwkv_k_a[l], rwkv_r_k[l], rwkv_ln_w[l], rwkv_ln_b[l],
                         w_out[l], ffn_w_gu[l], ffn_w_down[l])
        mod3 = _mod(cvec, w_mod[l], b_mod[l].reshape(1, -1)).reshape(8, 6, D_MODEL)
        y_p, lru_ctx, wkv_ctx = _group(
            y_p, mod3, lambda row: 0, False,
            jnp.zeros((nb_p, 2, LRU_WIDTH), F32), jnp.zeros((nb_p * PAIRS, 2, LANES, LANES), F32), P)
        new_lru.append(lru_ctx)
        new_wkv.append(_pair_diag_blocks(wkv_ctx, nb_p))
        y_s, _, _ = _group(
            y_s, mod3, lambda row: 1 + row // seq_s, True,
            state_lru[:, l], _block_diag_pairs(state_wkv[:, l]), P)
    return (y_p, y_s, jnp.stack(new_lru, axis=1), jnp.stack(new_wkv, axis=1))
```

```python
import functools

import jax
import jax.numpy as jnp
from jax import lax
from jax.experimental import pallas as pl
from jax.experimental.pallas import tpu as pltpu

F32 = jnp.float32
BF16 = jnp.bfloat16

D_MODEL = 2048
LRU_WIDTH = 1024
LRU_HEADS = 8
LRU_HEAD_DIM = 128
LRU_C = 8.0
RWKV_WIDTH = 1024
HEAD_SIZE = 64
RWKV_HEADS = 16
DECAY_LORA = 64
AAA_LORA = 64
GATE_LORA = 160
LORA_WIDTH = DECAY_LORA + AAA_LORA + GATE_LORA
RWKV_IN_WIDTH = 3 * RWKV_WIDTH + LORA_WIDTH
IN_WIDTH = 2 * LRU_WIDTH + RWKV_IN_WIDTH
FFN_HIDDEN = 5632
GRID_W = 64
RMS_EPS = 1e-6
GN_EPS = 64e-5

LANES = 128
PROJ_PAD = 5632
LORA_COL = 2 * LRU_WIDTH + 3 * RWKV_WIDTH
LORA_OUT = 384
CHUNK = 64
PAIRS = RWKV_HEADS // 2
VMEM_LIMIT = 56 * 1024 * 1024


def _cparams(sem, flags=None):
    return pltpu.CompilerParams(dimension_semantics=sem, vmem_limit_bytes=VMEM_LIMIT, flags=flags)


def _dot(a, b):
    return jnp.dot(a, b, preferred_element_type=F32)


def _dot_nt(a, b):
    return lax.dot_general(a, b, (((1,), (1,)), ((), ())), preferred_element_type=F32)


def _dot_tn(a, b):
    return lax.dot_general(a, b, (((0,), (0,)), ((), ())), preferred_element_type=F32)


def _split3(x):
    hi = x.astype(BF16)
    r1 = x - hi.astype(F32)
    mid = r1.astype(BF16)
    lo = (r1 - mid.astype(F32)).astype(BF16)
    return hi, mid, lo


def _dot_sel_l(sel, x):
    hi, mid, lo = _split3(x)
    return _dot(sel, hi) + _dot(sel, mid) + _dot(sel, lo)


def _dot_sel_r(x, sel):
    hi, mid, lo = _split3(x)
    return _dot(hi, sel) + _dot(mid, sel) + _dot(lo, sel)


def _softplus(x):
    return jnp.maximum(x, 0.0) + jnp.log1p(jnp.exp(-jnp.abs(x)))


def _rms(x, g):
    return x * lax.rsqrt(jnp.mean(x * x, axis=-1, keepdims=True) + RMS_EPS) * g


def _mod_kernel(c_ref, w_ref, b_ref, o_ref):
    c = c_ref[...]
    s = c * jax.nn.sigmoid(c)
    o_ref[...] = jnp.dot(s, w_ref[...], precision=lax.Precision.HIGHEST,
                         preferred_element_type=F32) + b_ref[...]


def _mod(cvec, w_mod, b_mod):
    n = w_mod.shape[1]
    tn = 1024
    return pl.pallas_call(
        _mod_kernel,
        grid=(n // tn,),
        in_specs=[pl.BlockSpec((8, D_MODEL), lambda j: (0, 0)),
                  pl.BlockSpec((D_MODEL, tn), lambda j: (0, j)),
                  pl.BlockSpec((1, tn), lambda j: (0, j))],
        out_specs=pl.BlockSpec((8, tn), lambda j: (0, j)),
        out_shape=jax.ShapeDtypeStruct((8, n), F32),
        compiler_params=_cparams(("parallel",)),
        name="mod",
    )(cvec, w_mod, b_mod)


def _in_proj_kernel(x_ref, mod_ref, g_ref, w_ref, o_ref, h_scr):
    @pl.when(pl.program_id(1) == 0)
    def _():
        m = mod_ref[0]
        h = _rms(x_ref[...], g_ref[...]) * (1.0 + m[1:2]) + m[0:1]
        h_scr[...] = h.astype(BF16)

    o_ref[...] = _dot(h_scr[...], w_ref[...])


def _in_proj(x2d, mod3, g, w_in_p, mod_idx, tm=512, tn=512):
    m = x2d.shape[0]
    return pl.pallas_call(
        _in_proj_kernel,
        grid=(m // tm, PROJ_PAD // tn),
        in_specs=[pl.BlockSpec((tm, D_MODEL), lambda i, j: (i, 0)),
                  pl.BlockSpec((1, 6, D_MODEL), lambda i, j: (mod_idx(i * tm), 0, 0)),
                  pl.BlockSpec((1, D_MODEL), lambda i, j: (0, 0)),
                  pl.BlockSpec((D_MODEL, tn), lambda i, j: (0, j))],
        out_specs=pl.BlockSpec((tm, tn), lambda i, j: (i, j)),
        out_shape=jax.ShapeDtypeStruct((m, PROJ_PAD), F32),
        scratch_shapes=[pltpu.VMEM((tm, D_MODEL), BF16)],
        compiler_params=_cparams(("parallel", "arbitrary")),
        name="in_proj",
    )(x2d, mod3, g, w_in_p)


def _lru_kernel(xl_ref, gl_ref, cw_ref, cb_ref, wg_ref, bg_ref, lam_ref, h0_ref,
                out_ref, hfin_ref, xpad, a_f, b_f, a_b, b_b, *, seq, tile):
    n_tiles = seq // tile
    zeros8 = jnp.zeros((8, LANES), F32)
    xpad[pl.ds(0, 8), :] = zeros8
    xpad[pl.ds(seq + 8, 8), :] = zeros8

    def copy_body(i, c):
        r0 = pl.multiple_of(i * tile, tile)
        xpad[pl.ds(r0 + 8, tile), :] = xl_ref[0, pl.ds(r0, tile), :]
        return c

    lax.fori_loop(0, n_tiles, copy_body, 0)

    cneg = -LRU_C * _softplus(-lam_ref[0])
    cw = cw_ref[...]
    cb = cb_ref[...]
    bg = bg_ref[0]

    def gate_body(i, c):
        r0 = pl.multiple_of(i * tile, tile)
        ext = xpad[pl.ds(r0, tile + 16), :]
        n_ext = tile + 16
        xc = (pltpu.roll(ext, 2, 0)[8:8 + tile] * cw[0:1]
              + pltpu.roll(ext, 1, 0)[8:8 + tile] * cw[1:2]
              + ext[8:8 + tile] * cw[2:3]
              + pltpu.roll(ext, n_ext - 1, 0)[8:8 + tile] * cw[3:4]) + cb
        g = _dot(xc.astype(BF16), wg_ref[0]) + bg
        for d, (a_s, b_s) in enumerate(((a_f, b_f), (a_b, b_b))):
            r = jax.nn.sigmoid(g[:, 256 * d:256 * d + 128])
            ig = jax.nn.sigmoid(g[:, 256 * d + 128:256 * d + 256])
            log_a = cneg[d:d + 1] * r
            a = jnp.exp(log_a)
            one_m_a2 = -jnp.tanh(log_a) * (a * a + 1.0)
            a_s[pl.ds(r0, tile), :] = a
            b_s[pl.ds(r0, tile), :] = jnp.sqrt(one_m_a2) * (ig * xc)
        return c

    lax.fori_loop(0, n_tiles, gate_body, 0)

    n8 = seq // 8

    def scan_body(i, carry):
        hf, hb = carry
        r0 = pl.multiple_of(i * 8, 8)
        a = a_f[pl.ds(r0, 8), :]
        b = b_f[pl.ds(r0, 8), :]
        rows = []
        for r in range(8):
            hf = a[r:r + 1] * hf + b[r:r + 1]
            rows.append(hf)
        b_f[pl.ds(r0, 8), :] = jnp.concatenate(rows, axis=0)
        r1 = pl.multiple_of((n8 - 1 - i) * 8, 8)
        a = a_b[pl.ds(r1, 8), :]
        b = b_b[pl.ds(r1, 8), :]
        rows = [None] * 8
        for r in range(7, -1, -1):
            hb = a[r:r + 1] * hb + b[r:r + 1]
            rows[r] = hb
        b_b[pl.ds(r1, 8), :] = jnp.concatenate(rows, axis=0)
        return hf, hb

    h0 = h0_ref[0]
    hf, hb = lax.fori_loop(0, n8, scan_body, (h0[0:1], h0[1:2]))
    hfin_ref[0] = jnp.concatenate([hf, hb], axis=0)

    def out_body(i, c):
        r0 = pl.multiple_of(i * tile, tile)
        gl = gl_ref[0, pl.ds(r0, tile), :]
        hs = b_f[pl.ds(r0, tile), :] + b_b[pl.ds(r0, tile), :]
        out_ref[0, pl.ds(r0, tile), :] = (hs * jax.nn.gelu(gl)).astype(BF16)
        return c

    lax.fori_loop(0, n_tiles, out_body, 0)


def _lru(proj3, conv_w, conv_b, wg, bg, lam, h0):
    nb, seq, _ = proj3.shape
    tile = 256
    kern = functools.partial(_lru_kernel, seq=seq, tile=tile)
    return pl.pallas_call(
        kern,
        grid=(nb, LRU_HEADS),
        in_specs=[pl.BlockSpec((1, seq, LANES), lambda b, h: (b, 0, h)),
                  pl.BlockSpec((1, seq, LANES), lambda b, h: (b, 0, LRU_HEADS + h)),
                  pl.BlockSpec((4, LANES), lambda b, h: (0, h)),
                  pl.BlockSpec((1, LANES), lambda b, h: (0, h)),
                  pl.BlockSpec((1, LANES, 512), lambda b, h: (h, 0, 0)),
                  pl.BlockSpec((1, 1, 512), lambda b, h: (h, 0, 0)),
                  pl.BlockSpec((1, 2, LANES), lambda b, h: (h, 0, 0)),
                  pl.BlockSpec((1, 2, LANES), lambda b, h: (b, 0, h))],
        out_specs=[pl.BlockSpec((1, seq, LANES), lambda b, h: (b, 0, h)),
                   pl.BlockSpec((1, 2, LANES), lambda b, h: (b, 0, h))],
        out_shape=[jax.ShapeDtypeStruct((nb, seq, LRU_WIDTH), BF16),
                   jax.ShapeDtypeStruct((nb, 2, LRU_WIDTH), F32)],
        scratch_shapes=[pltpu.VMEM((seq + 16, LANES), F32)] + [pltpu.VMEM((seq, LANES), F32)] * 4,
        compiler_params=_cparams(("parallel", "arbitrary")),
        name="lru",
    )(proj3, proj3, conv_w, conv_b, wg, bg, lam, h0)


def _shift_select(x_ref, xpad, r0, rows, seq, cg, grid_mode):
    cur = x_ref[0, pl.ds(r0, rows), :]
    if grid_mode:
        rid = lax.broadcasted_iota(jnp.int32, cur.shape, 0)
        left = jnp.where(rid == 0, 0.0, pltpu.roll(cur, 1, 0))
        right = jnp.where(rid == rows - 1, 0.0, pltpu.roll(cur, rows - 1, 0))
        up_r = pl.multiple_of(jnp.maximum(r0 - GRID_W, 0), GRID_W)
        dn_r = pl.multiple_of(jnp.minimum(r0 + GRID_W, seq - GRID_W), GRID_W)
        up = jnp.where(r0 > 0, x_ref[0, pl.ds(up_r, rows), :], 0.0)
        down = jnp.where(r0 < seq - GRID_W, x_ref[0, pl.ds(dn_r, rows), :], 0.0)
        q = RWKV_IN_WIDTH // 4
        sh = jnp.where(cg < q, left, jnp.where(cg < 2 * q, right, jnp.where(cg < 3 * q, up, down)))
    else:
        ext = xpad[pl.ds(r0, rows + 16), :]
        prev = pltpu.roll(ext, 1, 0)[8:8 + rows]
        nxt = pltpu.roll(ext, rows + 15, 0)[8:8 + rows]
        sh = jnp.where(cg < RWKV_IN_WIDTH // 2, prev, nxt)
    return cur, sh


def _fill_xpad(x_ref, xpad, seq, tile):
    width = xpad.shape[1]
    zeros8 = jnp.zeros((8, width), F32)
    xpad[pl.ds(0, 8), :] = zeros8
    xpad[pl.ds(seq + 8, 8), :] = zeros8

    def copy_body(i, c):
        r0 = pl.multiple_of(i * tile, tile)
        xpad[pl.ds(r0 + 8, tile), :] = x_ref[0, pl.ds(r0, tile), :]
        return c

    lax.fori_loop(0, seq // tile, copy_body, 0)


def _shift_main_kernel(x_ref, mu_ref, o_ref, xpad, *, seq, grid_mode):
    rows = GRID_W if grid_mode else 256
    if not grid_mode:
        _fill_xpad(x_ref, xpad, seq, rows)
    mu = mu_ref[...]
    col0 = pl.program_id(1) * LANES

    def body(i, c):
        r0 = pl.multiple_of(i * rows, rows)
        cg = col0 + lax.broadcasted_iota(jnp.int32, (rows, LANES), 1)
        cur, sh = _shift_select(x_ref, xpad, r0, rows, seq, cg, grid_mode)
        o_ref[0, pl.ds(r0, rows), :] = cur + mu * (sh - cur)
        return c

    lax.fori_loop(0, seq // rows, body, 0)


def _shift_main(proj3, mu_main, grid_mode):
    nb, seq, _ = proj3.shape
    nblk = 3 * RWKV_WIDTH // LANES
    first = 2 * LRU_WIDTH // LANES
    kern = functools.partial(_shift_main_kernel, seq=seq, grid_mode=grid_mode)
    return pl.pallas_call(
        kern,
        grid=(nb, nblk),
        in_specs=[pl.BlockSpec((1, seq, LANES), lambda b, j: (b, 0, first + j)),
                  pl.BlockSpec((1, LANES), lambda b, j: (0, j))],
        out_specs=pl.BlockSpec((1, seq, LANES), lambda b, j: (b, 0, j)),
        out_shape=jax.ShapeDtypeStruct((nb, seq, 3 * RWKV_WIDTH), F32),
        scratch_shapes=[pltpu.VMEM((8, LANES) if grid_mode else (seq + 16, LANES), F32)],
        compiler_params=_cparams(("parallel", "parallel")),
        name="shift_main",
    )(proj3, mu_main)


def _shift_lora_kernel(x_ref, mu_ref, o_ref, xpad, *, seq, grid_mode):
    rows = GRID_W if grid_mode else 256
    if not grid_mode:
        _fill_xpad(x_ref, xpad, seq, rows)
    mu = mu_ref[...]

    def body(i, c):
        r0 = pl.multiple_of(i * rows, rows)
        cg = 3 * RWKV_WIDTH + lax.broadcasted_iota(jnp.int32, (rows, 512), 1)
        cur, sh = _shift_select(x_ref, xpad, r0, rows, seq, cg, grid_mode)
        y = (cur + mu * (sh - cur))[:, :LORA_OUT]
        lane = lax.broadcasted_iota(jnp.int32, y.shape, 1)
        act = jnp.where(lane < DECAY_LORA, jnp.tanh(y),
                        jnp.where(lane < DECAY_LORA + AAA_LORA, y, jax.nn.sigmoid(y)))
        o_ref[0, pl.ds(r0, rows), :] = act.astype(BF16)
        return c

    lax.fori_loop(0, seq // rows, body, 0)


def _shift_lora(proj3, mu_lora, grid_mode):
    nb, seq, _ = proj3.shape
    kern = functools.partial(_shift_lora_kernel, seq=seq, grid_mode=grid_mode)
    return pl.pallas_call(
        kern,
        grid=(nb,),
        in_specs=[pl.BlockSpec((1, seq, 512), lambda b: (b, 0, LORA_COL // 512)),
                  pl.BlockSpec((1, 512), lambda b: (0, 0))],
        out_specs=pl.BlockSpec((1, seq, LORA_OUT), lambda b: (b, 0, 0)),
        out_shape=jax.ShapeDtypeStruct((nb, seq, LORA_OUT), BF16),
        scratch_shapes=[pltpu.VMEM((8, 512) if grid_mode else (seq + 16, 512), F32)],
        compiler_params=_cparams(("parallel",)),
        name="shift_lora",
    )(proj3, mu_lora)


_V_W0, _V_A0, _V_KK, _V_KA, _V_RK, _V_LNW, _V_LNB = 0, 2, 4, 5, 6, 7, 8


def _wkv_kernel(r_ref, k_ref, v_ref, lo_ref, ww_ref, wa_ref, wgt_ref, vec_ref, s0_ref,
                out_ref, sfin_ref, q_s, lr_s, g_s, gam_s, ybuf, *, seq, unroll):
    n_chunks = seq // CHUNK
    C = CHUNK
    vec = vec_ref[0]
    lane = lax.broadcasted_iota(jnp.int32, (C, LANES), 1)
    head0 = lane < HEAD_SIZE
    ri = lax.broadcasted_iota(jnp.int32, (LANES, LANES), 0)
    ci = lax.broadcasted_iota(jnp.int32, (LANES, LANES), 1)
    seg_ones = ((ri // HEAD_SIZE) == (ci // HEAD_SIZE)).astype(BF16)
    eye = (ri == ci).astype(F32)
    tr = lax.broadcasted_iota(jnp.int32, (C, C), 0)
    tc = lax.broadcasted_iota(jnp.int32, (C, C), 1)
    tri = ((tc <= tr).astype(BF16), (tc >= tr).astype(BF16))
    br = lax.broadcasted_iota(jnp.int32, (2 * LANES, 2 * LANES), 0)
    bc = lax.broadcasted_iota(jnp.int32, (2 * LANES, 2 * LANES), 1)
    brt, bct = br % C, bc % C
    incl = (br >= LANES).astype(jnp.int32)
    big_mask = (bct < brt + incl, bct > brt - incl)

    def stack(x):
        return jnp.concatenate([jnp.where(head0, x, 0.0), jnp.where(head0, 0.0, x)], axis=0).astype(BF16)

    def each(fn, *lists):
        return [fn(*xs) for xs in zip(*lists)]

    def bf(xs):
        return [x.astype(BF16) for x in xs]

    def prep_group(chunks):
        rows = [pl.multiple_of(c * C, C) for c in chunks]
        r = [r_ref[0, pl.ds(r0, C), :] for r0 in rows]
        k = [k_ref[0, pl.ds(r0, C), :] for r0 in rows]
        v = [v_ref[0, pl.ds(r0, C), :] for r0 in rows]
        lo = [lo_ref[0, pl.ds(r0, C), 0:LANES] for r0 in rows]
        lw = each(lambda x: _dot(x, ww_ref[0]), lo)
        la = each(lambda x: _dot(x, wa_ref[0]), lo)
        kk = each(lambda x: x * vec[_V_KK:_V_KK + 1], k)
        ss = each(lambda x: _dot_sel_r(x * x, seg_ones), kk)
        kk = each(lambda x, s: x * lax.rsqrt(jnp.maximum(s, 1e-24)), kk, ss)
        bonus = each(lambda rr, kx, vx: _dot_sel_r(rr * kx * vec[_V_RK:_V_RK + 1], seg_ones) * vx, r, k, v)
        vs = each(stack, v)
        ch = [(i, d) for i in range(len(chunks)) for d in range(2)]
        logw = [-jnp.exp(-_softplus(-(vec[_V_W0 + d:_V_W0 + d + 1] + lw[i][:, d * LANES:(d + 1) * LANES])) - 0.5)
                for i, d in ch]
        a = [jax.nn.sigmoid(vec[_V_A0 + d:_V_A0 + d + 1] + la[i][:, d * LANES:(d + 1) * LANES]) for i, d in ch]
        kd = [k[i] * (1.0 + (ax - 1.0) * vec[_V_KA:_V_KA + 1]) for (i, d), ax in zip(ch, a)]
        bt = [kk[i] * ax for (i, d), ax in zip(ch, a)]
        cl = [_dot_sel_l(tri[d], lx) for (i, d), lx in zip(ch, logw)]
        ltot = each(lambda lx: jnp.sum(lx, axis=0, keepdims=True), logw)
        e_neg = each(lambda x: jnp.exp(-x), cl)
        e_tot = each(lambda t, x: jnp.exp(t - x), ltot, cl)
        ar = [jnp.concatenate([stack(-kk[i] * jnp.exp(cx - lx)), stack(r[i] * jnp.exp(cx))], axis=0)
              for (i, d), cx, lx in zip(ch, cl, logw)]
        bk = each(lambda b, kx, e: jnp.concatenate([stack(b * e), stack(kx * e)], axis=0), bt, kd, e_neg)
        bkc = each(lambda b, kx, e: jnp.concatenate([stack(b * e), stack(kx * e)], axis=0), bt, kd, e_tot)
        big = [jnp.where(big_mask[d], _dot_nt(x, y), 0.0) for (i, d), x, y in zip(ch, ar, bk)]
        n1 = [b[0:LANES, 0:LANES] for b in big]
        n1b = bf(n1)
        xyv = [_dot(b[:, LANES:].astype(BF16), vs[i]) for (i, d), b in zip(ch, big)]
        n2 = each(_dot, n1b, n1b)
        n2b = bf(n2)
        n4 = each(_dot, n2b, n2b)
        p1 = each(lambda x1, x2, x1b, x2b: eye + x1 + x2 + _dot(x1b, x2b), n1, n2, n1b, n2b)
        n4b = bf(n4)
        n8 = each(_dot, n4b, n4b)
        n8b = bf(n8)
        n16 = each(_dot, n8b, n8b)
        p2 = each(lambda x1, x2, x1b, x2b: eye + x1 + x2 + _dot(x1b, x2b), n4, n8, n4b, n8b)
        n16b = bf(n16)
        n32 = each(_dot, n16b, n16b)
        p12 = each(_dot, bf(p1), bf(p2))
        p3 = each(lambda x1, x2, x1b, x2b: eye + x1 + x2 + _dot(x1b, x2b), n16, n32, n16b, bf(n32))
        tb = bf(each(_dot, bf(p12), bf(p3)))
        aub = bf(each(lambda t, x, y: _dot(t, jnp.concatenate([x[0:LANES], y[0:LANES].astype(BF16)], axis=1)),
                      tb, ar, xyv))
        qy = each(lambda b, x: _dot(b[LANES:, 0:LANES].astype(BF16), x), big, aub)
        lr = each(lambda x, y: _dot_tn(x[:, 0:LANES], y[0:LANES]), aub, bkc)
        g = [_dot_tn(jnp.concatenate([x[:, LANES:], vs[i]], axis=0), y) for (i, d), x, y in zip(ch, aub, bkc)]
        for (i, d), x, q, l, gx, t in zip(ch, ar, qy, lr, g, ltot):
            c = chunks[i]
            q_s[d, c] = (x[LANES:].astype(F32) + q[:, 0:LANES]).astype(BF16)
            lr_s[d, c] = l.astype(BF16)
            g_s[d, c] = gx
            gam_s[d, c] = jnp.broadcast_to(jnp.exp(t), (8, LANES))
        y0 = each(lambda x, q: x[LANES:] + q[:, LANES:], xyv, qy)
        for i, r0 in enumerate(rows):
            y = bonus[i]
            for d in range(2):
                yd = y0[2 * i + d]
                y = y + yd[0:C] + yd[C:]
            ybuf[pl.ds(r0, C), :] = y

    def prep_body(i, carry):
        prep_group([i * unroll + u for u in range(unroll)])
        return carry

    lax.fori_loop(0, n_chunks // unroll, prep_body, 0)

    def step_body(i, carry):
        cs = (i, n_chunks - 1 - i)
        sb = bf(carry)
        ys = [_dot_nt(q_s[d, cs[d]], sb[d]) for d in range(2)]
        ls = [_dot(sb[d], lr_s[d, cs[d]]) for d in range(2)]
        for d in range(2):
            r0 = pl.multiple_of(cs[d] * C, C)
            ybuf[pl.ds(r0, C), :] += ys[d][0:C] + ys[d][C:]
        return tuple(carry[d] * gam_s[d, cs[d]][0:1] + ls[d] + g_s[d, cs[d]] for d in range(2))

    sf, sb = lax.fori_loop(0, n_chunks, step_body, (s0_ref[0, 0], s0_ref[0, 1]))
    sfin_ref[0, 0] = sf
    sfin_ref[0, 1] = sb

    def out_body(i, carry):
        rows = [pl.multiple_of((i * unroll + u) * C, C) for u in range(unroll)]
        y = [ybuf[pl.ds(r0, C), :] for r0 in rows]
        g = [_dot(lo_ref[0, pl.ds(r0, C), LANES:LORA_OUT], wgt_ref[0]) for r0 in rows]
        mean = each(lambda x: _dot_sel_r(x, seg_ones) * (1.0 / HEAD_SIZE), y)
        dlt = each(lambda x, m: x - m, y, mean)
        var = each(lambda x: _dot_sel_r(x * x, seg_ones) * (1.0 / HEAD_SIZE), dlt)
        for r0, dx, vx, gx in zip(rows, dlt, var, g):
            yn = dx * lax.rsqrt(vx + GN_EPS) * vec[_V_LNW:_V_LNW + 1] + vec[_V_LNB:_V_LNB + 1]
            out_ref[0, pl.ds(r0, C), :] = (yn * gx).astype(BF16)
        return carry

    lax.fori_loop(0, n_chunks // unroll, out_body, 0)


def _wkv(prs, lora, ww, wa, wgt, vec, s0):
    nb, seq, _ = prs.shape
    n_chunks = seq // CHUNK
    kern = functools.partial(_wkv_kernel, seq=seq, unroll=4)
    nblk = RWKV_WIDTH // LANES
    return pl.pallas_call(
        kern,
        grid=(nb, PAIRS),
        in_specs=[pl.BlockSpec((1, seq, LANES), lambda b, j: (b, 0, j)),
                  pl.BlockSpec((1, seq, LANES), lambda b, j: (b, 0, nblk + j)),
                  pl.BlockSpec((1, seq, LANES), lambda b, j: (b, 0, 2 * nblk + j)),
                  pl.BlockSpec((1, seq, LORA_OUT), lambda b, j: (b, 0, 0)),
                  pl.BlockSpec((1, LANES, 2 * LANES), lambda b, j: (j, 0, 0)),
                  pl.BlockSpec((1, LANES, 2 * LANES), lambda b, j: (j, 0, 0)),
                  pl.BlockSpec((1, 2 * LANES, LANES), lambda b, j: (j, 0, 0)),
                  pl.BlockSpec((1, 16, LANES), lambda b, j: (j, 0, 0)),
                  pl.BlockSpec((1, 2, LANES, LANES), lambda b, j: (b * PAIRS + j, 0, 0, 0))],
        out_specs=[pl.BlockSpec((1, seq, LANES), lambda b, j: (b, 0, j)),
                   pl.BlockSpec((1, 2, LANES, LANES), lambda b, j: (b * PAIRS + j, 0, 0, 0))],
        out_shape=[jax.ShapeDtypeStruct((nb, seq, RWKV_WIDTH), BF16),
                   jax.ShapeDtypeStruct((nb * PAIRS, 2, LANES, LANES), F32)],
        scratch_shapes=[pltpu.VMEM((2, n_chunks, LANES, LANES), BF16),
                        pltpu.VMEM((2, n_chunks, LANES, LANES), BF16),
                        pltpu.VMEM((2, n_chunks, LANES, LANES), F32),
                        pltpu.VMEM((2, n_chunks, 8, LANES), F32),
                        pltpu.VMEM((seq, LANES), F32)],
        compiler_params=_cparams(("parallel", "arbitrary")),
        name="wkv",
    )(prs, prs, prs, lora, ww, wa, wgt, vec, s0)


def _out_proj_kernel(lru_ref, rw_ref, w1_ref, w2_ref, x_ref, mod_ref, g_ref, x1_ref, h2_ref):
    out = _dot(lru_ref[...], w1_ref[...]) + _dot(rw_ref[...], w2_ref[...])
    m = mod_ref[0]
    g = g_ref[...]
    x1 = x_ref[...] + m[2:3] * _rms(out, g[0:1])
    x1_ref[...] = x1
    h2_ref[...] = (_rms(x1, g[1:2]) * (1.0 + m[4:5]) + m[3:4]).astype(BF16)


def _out_proj(lru_o, rw_o, w_out_b, x2d, mod3, g2, mod_idx, tm=512):
    m = x2d.shape[0]
    half = D_MODEL // 2
    return pl.pallas_call(
        _out_proj_kernel,
        grid=(m // tm,),
        in_specs=[pl.BlockSpec((tm, half), lambda i: (i, 0)),
                  pl.BlockSpec((tm, half), lambda i: (i, 0)),
                  pl.BlockSpec((half, D_MODEL), lambda i: (0, 0)),
                  pl.BlockSpec((half, D_MODEL), lambda i: (1, 0)),
                  pl.BlockSpec((tm, D_MODEL), lambda i: (i, 0)),
                  pl.BlockSpec((1, 6, D_MODEL), lambda i: (mod_idx(i * tm), 0, 0)),
                  pl.BlockSpec((2, D_MODEL), lambda i: (0, 0))],
        out_specs=[pl.BlockSpec((tm, D_MODEL), lambda i: (i, 0)),
                   pl.BlockSpec((tm, D_MODEL), lambda i: (i, 0))],
        out_shape=[jax.ShapeDtypeStruct((m, D_MODEL), F32),
                   jax.ShapeDtypeStruct((m, D_MODEL), BF16)],
        compiler_params=_cparams(("parallel",)),
        name="out_proj",
    )(lru_o, rw_o, w_out_b, w_out_b, x2d, mod3, g2)


def _ffn_kernel(h_ref, wg_ref, wu_ref, wd_ref, x1_ref, mod_ref, g_ref, o_ref, acc):
    f = pl.program_id(1)
    h = h_ref[...]
    gate = _dot(h, wg_ref[...])
    up = _dot(h, wu_ref[...])
    act = (gate * jax.nn.sigmoid(gate) * up).astype(BF16)
    part = _dot(act, wd_ref[...])

    @pl.when(f == 0)
    def _():
        acc[...] = part

    @pl.when(f > 0)
    def _():
        acc[...] += part

    @pl.when(f == pl.num_programs(1) - 1)
    def _():
        o_ref[...] = x1_ref[...] + mod_ref[0][5:6] * _rms(acc[...], g_ref[...])


def _ffn(h2, w_gu_b, w_down_b, x1, mod3, g, mod_idx, tm=512, tf=512):
    m = h2.shape[0]
    nf = FFN_HIDDEN // tf
    return pl.pallas_call(
        _ffn_kernel,
        grid=(m // tm, nf),
        in_specs=[pl.BlockSpec((tm, D_MODEL), lambda i, f: (i, 0)),
                  pl.BlockSpec((D_MODEL, tf), lambda i, f: (0, f)),
                  pl.BlockSpec((D_MODEL, tf), lambda i, f: (0, nf + f)),
                  pl.BlockSpec((tf, D_MODEL), lambda i, f: (f, 0)),
                  pl.BlockSpec((tm, D_MODEL), lambda i, f: (i, 0)),
                  pl.BlockSpec((1, 6, D_MODEL), lambda i, f: (mod_idx(i * tm), 0, 0)),
                  pl.BlockSpec((1, D_MODEL), lambda i, f: (0, 0))],
        out_specs=pl.BlockSpec((tm, D_MODEL), lambda i, f: (i, 0)),
        out_shape=jax.ShapeDtypeStruct((m, D_MODEL), F32),
        scratch_shapes=[pltpu.VMEM((tm, D_MODEL), F32)],
        compiler_params=_cparams(("parallel", "arbitrary")),
        name="ffn",
    )(h2, w_gu_b, w_gu_b, w_down_b, x1, mod3, g)


def _block_diag_pairs(s):
    nb = s.shape[0]
    s = s.reshape(nb, 2, PAIRS, 2, HEAD_SIZE, HEAD_SIZE)
    z = jnp.zeros((nb, 2, PAIRS, 2, HEAD_SIZE, 2, HEAD_SIZE), F32)
    z = z.at[:, :, :, 0, :, 0, :].set(s[:, :, :, 0]).at[:, :, :, 1, :, 1, :].set(s[:, :, :, 1])
    z = z.reshape(nb, 2, PAIRS, LANES, LANES)
    return jnp.swapaxes(z, 1, 2).reshape(nb * PAIRS, 2, LANES, LANES)


def _pair_diag_blocks(sbd, nb):
    z = sbd.reshape(nb, PAIRS, 2, 2, HEAD_SIZE, 2, HEAD_SIZE)
    heads = jnp.stack([z[:, :, :, 0, :, 0, :], z[:, :, :, 1, :, 1, :]], axis=3)
    return jnp.swapaxes(heads, 1, 2).reshape(nb, 2, RWKV_HEADS, HEAD_SIZE, HEAD_SIZE)


def _group(x3, mod3, mod_idx, grid_mode, lru_h0, wkv_s0, P):
    nb, seq, _ = x3.shape
    x2d = x3.reshape(nb * seq, D_MODEL)
    proj = _in_proj(x2d, mod3, P["norm_mix_pre"], P["w_in"], mod_idx).reshape(nb, seq, PROJ_PAD)
    lru_o, lru_fin = _lru(proj, P["conv_w"], P["conv_b"], P["lru_wg"], P["lru_bg"], P["lru_lam"], lru_h0)
    prs = _shift_main(proj, P["mu_main"], grid_mode)
    lora = _shift_lora(proj, P["mu_lora"], grid_mode)
    rw_o, s_fin = _wkv(prs, lora, P["ww"], P["wa"], P["wgt"], P["vec"], wkv_s0)
    x1, h2 = _out_proj(lru_o.reshape(nb * seq, LRU_WIDTH), rw_o.reshape(nb * seq, RWKV_WIDTH),
                       P["w_out"], x2d, mod3, P["g_post_pre"], mod_idx)
    y = _ffn(h2, P["w_gu"], P["w_down"], x1, mod3, P["norm_ffn_post"], mod_idx)
    return y.reshape(nb, seq, D_MODEL), lru_fin, s_fin


def _prep_params(norm_mix_pre, norm_mix_post, norm_ffn_pre, norm_ffn_post, w_in,
                 lru_conv_w, lru_conv_b, lru_wr, lru_br, lru_wi, lru_bi, lru_lambda,
                 rwkv_mu, rwkv_w0, rwkv_w_up, rwkv_a0, rwkv_a_up, rwkv_g_up, rwkv_k_k, rwkv_k_a, rwkv_r_k,
                 rwkv_ln_w, rwkv_ln_b, w_out, ffn_w_gu, ffn_w_down):
    P = {}
    P["norm_mix_pre"] = norm_mix_pre.reshape(1, D_MODEL)
    P["g_post_pre"] = jnp.stack([norm_mix_post, norm_ffn_pre], axis=0)
    P["norm_ffn_post"] = norm_ffn_post.reshape(1, D_MODEL)
    P["w_in"] = jnp.pad(w_in.astype(BF16), ((0, 0), (0, PROJ_PAD - IN_WIDTH)))
    P["conv_w"] = lru_conv_w
    P["conv_b"] = lru_conv_b.reshape(1, LRU_WIDTH)
    P["lru_wg"] = jnp.concatenate([lru_wr[0], lru_wi[0], lru_wr[1], lru_wi[1]], axis=-1).astype(BF16)
    P["lru_bg"] = jnp.concatenate([lru_br[0], lru_bi[0], lru_br[1], lru_bi[1]], axis=-1).reshape(LRU_HEADS, 1, 512)
    P["lru_lam"] = jnp.swapaxes(lru_lambda.reshape(2, LRU_HEADS, LRU_HEAD_DIM), 0, 1)
    P["mu_main"] = rwkv_mu[:3 * RWKV_WIDTH].reshape(1, 3 * RWKV_WIDTH)
    P["mu_lora"] = jnp.pad(rwkv_mu[3 * RWKV_WIDTH:], (0, 512 - LORA_WIDTH)).reshape(1, 512)

    def per_pair(w):
        return jnp.swapaxes(w.reshape(w.shape[0], PAIRS, LANES), 0, 1)

    wu = jnp.concatenate([per_pair(rwkv_w_up[0]), per_pair(rwkv_w_up[1])], axis=-1)
    au = jnp.concatenate([per_pair(rwkv_a_up[0]), per_pair(rwkv_a_up[1])], axis=-1)
    P["ww"] = jnp.pad(wu, ((0, 0), (0, LANES - DECAY_LORA), (0, 0))).astype(BF16)
    P["wa"] = jnp.pad(au, ((0, 0), (DECAY_LORA, LANES - DECAY_LORA - AAA_LORA), (0, 0))).astype(BF16)
    P["wgt"] = jnp.pad(per_pair(rwkv_g_up), ((0, 0), (0, 2 * LANES - GATE_LORA), (0, 0))).astype(BF16)
    rows = [rwkv_w0[0], rwkv_w0[1], rwkv_a0[0], rwkv_a0[1], rwkv_k_k, rwkv_k_a,
            rwkv_r_k.reshape(RWKV_WIDTH), rwkv_ln_w, rwkv_ln_b]
    vec = jnp.stack(rows + [jnp.zeros_like(rwkv_k_k)] * (16 - len(rows)), axis=0)
    P["vec"] = per_pair(vec)
    P["w_out"] = w_out.astype(BF16)
    P["w_gu"] = ffn_w_gu.astype(BF16)
    P["w_down"] = ffn_w_down.astype(BF16)
    return P


def kernel(x_prompt, x_sample, state_lru, state_wkv, c, c_ctx, norm_mix_pre, norm_mix_post, norm_ffn_pre,
           norm_ffn_post, w_mod, b_mod, w_in, lru_conv_w, lru_conv_b, lru_wr, lru_br, lru_wi, lru_bi,
           lru_lambda, rwkv_mu, rwkv_w0, rwkv_w_up, rwkv_a0, rwkv_a_up, rwkv_g_up, rwkv_k_k, rwkv_k_a,
           rwkv_r_k, rwkv_ln_w, rwkv_ln_b, w_out, ffn_w_gu, ffn_w_down):
    depth = w_in.shape[0]
    nb_p = x_prompt.shape[0]
    nb_s, seq_s, _ = x_sample.shape
    y_p, y_s = x_prompt, x_sample
    new_lru, new_wkv = [], []
    cvec = jnp.concatenate([c_ctx[None, :], c, jnp.zeros((8 - 1 - nb_s, D_MODEL), F32)], axis=0)
    for l in range(depth):
        P = _prep_params(norm_mix_pre[l], norm_mix_post[l], norm_ffn_pre[l], norm_ffn_post[l], w_in[l],
                         lru_conv_w[l], lru_conv_b[l], lru_wr[l], lru_br[l], lru_wi[l], lru_bi[l],
                         lru_lambda[l], rwkv_mu[l], rwkv_w0[l], rwkv_w_up[l], rwkv_a0[l], rwkv_a_up[l],
                         rwkv_g_up[l], rwkv_k_k[l], rwkv_k_a[l], rwkv_r_k[l], rwkv_ln_w[l], rwkv_ln_b[l],
                         w_out[l], ffn_w_gu[l], ffn_w_down[l])
        mod3 = _mod(cvec, w_mod[l], b_mod[l].reshape(1, -1)).reshape(8, 6, D_MODEL)
        y_p, lru_ctx, wkv_ctx = _group(
            y_p, mod3, lambda row: 0, False,
            jnp.zeros((nb_p, 2, LRU_WIDTH), F32), jnp.zeros((nb_p * PAIRS, 2, LANES, LANES), F32), P)
        new_lru.append(lru_ctx)
        new_wkv.append(_pair_diag_blocks(wkv_ctx, nb_p))
        y_s, _, _ = _group(
            y_s, mod3, lambda row: 1 + row // seq_s, True,
            state_lru[:, l], _block_diag_pairs(state_wkv[:, l]), P)
    return (y_p, y_s, jnp.stack(new_lru, axis=1), jnp.stack(new_wkv, axis=1))
```

```python
import functools

import jax
import jax.numpy as jnp
from jax import lax
from jax.experimental import pallas as pl
from jax.experimental.pallas import tpu as pltpu

F32 = jnp.float32
BF16 = jnp.bfloat16

D_MODEL = 2048
LRU_WIDTH = 1024
LRU_HEADS = 8
LRU_HEAD_DIM = 128
LRU_C = 8.0
RWKV_WIDTH = 1024
HEAD_SIZE = 64
RWKV_HEADS = 16
DECAY_LORA = 64
AAA_LORA = 64
GATE_LORA = 160
LORA_WIDTH = DECAY_LORA + AAA_LORA + GATE_LORA
RWKV_IN_WIDTH = 3 * RWKV_WIDTH + LORA_WIDTH
IN_WIDTH = 2 * LRU_WIDTH + RWKV_IN_WIDTH
FFN_HIDDEN = 5632
GRID_W = 64
RMS_EPS = 1e-6
GN_EPS = 64e-5

LANES = 128
LORA_OUT = 384
LORA_PAD = 512
PROJ_PAD = LORA_PAD + 2 * LRU_WIDTH + 3 * RWKV_WIDTH
XL_BLK = LORA_PAD // 128
GL_BLK = XL_BLK + LRU_WIDTH // 128
R_BLK = GL_BLK + LRU_WIDTH // 128
K_BLK = R_BLK + RWKV_WIDTH // 128
V_BLK = K_BLK + RWKV_WIDTH // 128
CHUNK = 64
PAIRS = RWKV_HEADS // 2
VMEM_LIMIT = 56 * 1024 * 1024


def _cparams(sem, flags=None):
    return pltpu.CompilerParams(dimension_semantics=sem, vmem_limit_bytes=VMEM_LIMIT, flags=flags)


def _dot(a, b):
    return jnp.dot(a, b, preferred_element_type=F32)


def _dot_nt(a, b):
    return lax.dot_general(a, b, (((1,), (1,)), ((), ())), preferred_element_type=F32)


def _dot_tn(a, b):
    return lax.dot_general(a, b, (((0,), (0,)), ((), ())), preferred_element_type=F32)


def _split3(x):
    hi = x.astype(BF16)
    r1 = x - hi.astype(F32)
    mid = r1.astype(BF16)
    lo = (r1 - mid.astype(F32)).astype(BF16)
    return hi, mid, lo


def _dot_sel_l(sel, x):
    hi, mid, lo = _split3(x)
    return _dot(sel, hi) + _dot(sel, mid) + _dot(sel, lo)


def _dot_sel_r(x, sel):
    hi, mid, lo = _split3(x)
    return _dot(hi, sel) + _dot(mid, sel) + _dot(lo, sel)


def _softplus(x):
    return jnp.maximum(x, 0.0) + jnp.log1p(jnp.exp(-jnp.abs(x)))


def _rms(x, g):
    return x * lax.rsqrt(jnp.mean(x * x, axis=-1, keepdims=True) + RMS_EPS) * g


def _mod_kernel(c_ref, w_ref, b_ref, o_ref):
    c = c_ref[...]
    s = c * jax.nn.sigmoid(c)
    o_ref[...] = jnp.dot(s, w_ref[...], precision=lax.Precision.HIGHEST,
                         preferred_element_type=F32) + b_ref[...]


def _mod(cvec, w_mod, b_mod):
    n = w_mod.shape[1]
    tn = 1024
    return pl.pallas_call(
        _mod_kernel,
        grid=(n // tn,),
        in_specs=[pl.BlockSpec((8, D_MODEL), lambda j: (0, 0)),
                  pl.BlockSpec((D_MODEL, tn), lambda j: (0, j)),
                  pl.BlockSpec((1, tn), lambda j: (0, j))],
        out_specs=pl.BlockSpec((8, tn), lambda j: (0, j)),
        out_shape=jax.ShapeDtypeStruct((8, n), F32),
        compiler_params=_cparams(("parallel",)),
        name="mod",
    )(cvec, w_mod, b_mod)


def _in_proj_kernel(x_ref, mod_ref, g_ref, w_ref, o_ref, h_scr):
    @pl.when(pl.program_id(1) == 0)
    def _():
        m = mod_ref[0]
        h = _rms(x_ref[...], g_ref[...]) * (1.0 + m[1:2]) + m[0:1]
        h_scr[...] = h.astype(BF16)

    o_ref[...] = _dot(h_scr[...], w_ref[...])


def _in_proj(x2d, mod3, g, w_in_p, mod_idx, tm=1024, tn=1408):
    m = x2d.shape[0]
    return pl.pallas_call(
        _in_proj_kernel,
        grid=(m // tm, PROJ_PAD // tn),
        in_specs=[pl.BlockSpec((tm, D_MODEL), lambda i, j: (i, 0)),
                  pl.BlockSpec((1, 6, D_MODEL), lambda i, j: (mod_idx(i * tm), 0, 0)),
                  pl.BlockSpec((1, D_MODEL), lambda i, j: (0, 0)),
                  pl.BlockSpec((D_MODEL, tn), lambda i, j: (0, j))],
        out_specs=pl.BlockSpec((tm, tn), lambda i, j: (i, j)),
        out_shape=jax.ShapeDtypeStruct((m, PROJ_PAD), F32),
        scratch_shapes=[pltpu.VMEM((tm, D_MODEL), BF16)],
        compiler_params=_cparams(("parallel", "arbitrary")),
        name="in_proj",
    )(x2d, mod3, g, w_in_p)


def _lru_kernel(xl_ref, gl_ref, cw_ref, cb_ref, wg_ref, bg_ref, lam_ref, h0_ref,
                out_ref, hfin_ref, xpad, a_f, b_f, a_b, b_b, *, seq, tile):
    n_tiles = seq // tile
    zeros8 = jnp.zeros((8, LANES), F32)
    xpad[pl.ds(0, 8), :] = zeros8
    xpad[pl.ds(seq + 8, 8), :] = zeros8

    def copy_body(i, c):
        r0 = pl.multiple_of(i * tile, tile)
        xpad[pl.ds(r0 + 8, tile), :] = xl_ref[0, pl.ds(r0, tile), :]
        return c

    lax.fori_loop(0, n_tiles, copy_body, 0)

    cneg = -LRU_C * _softplus(-lam_ref[0])
    cw = cw_ref[...]
    cb = cb_ref[...]
    bg = bg_ref[0]

    def gate_body(i, c):
        r0 = pl.multiple_of(i * tile, tile)
        ext = xpad[pl.ds(r0, tile + 16), :]
        n_ext = tile + 16
        xc = (pltpu.roll(ext, 2, 0)[8:8 + tile] * cw[0:1]
              + pltpu.roll(ext, 1, 0)[8:8 + tile] * cw[1:2]
              + ext[8:8 + tile] * cw[2:3]
              + pltpu.roll(ext, n_ext - 1, 0)[8:8 + tile] * cw[3:4]) + cb
        g = _dot(xc.astype(BF16), wg_ref[0]) + bg
        for d, (a_s, b_s) in enumerate(((a_f, b_f), (a_b, b_b))):
            r = jax.nn.sigmoid(g[:, 256 * d:256 * d + 128])
            ig = jax.nn.sigmoid(g[:, 256 * d + 128:256 * d + 256])
            log_a = cneg[d:d + 1] * r
            a = jnp.exp(log_a)
            one_m_a2 = -jnp.tanh(log_a) * (a * a + 1.0)
            a_s[pl.ds(r0, tile), :] = a
            b_s[pl.ds(r0, tile), :] = jnp.sqrt(one_m_a2) * (ig * xc)
        return c

    lax.fori_loop(0, n_tiles, gate_body, 0)

    group = 4
    n_steps = seq // (8 * group)
    rid = lax.broadcasted_iota(jnp.int32, (8, LANES), 0)

    def tile_scan(a, b, reverse):
        for s in (1, 2, 4):
            if reverse:
                keep = rid < 8 - s
                a_sh = jnp.where(keep, pltpu.roll(a, 8 - s, 0), 1.0)
                b_sh = jnp.where(keep, pltpu.roll(b, 8 - s, 0), 0.0)
            else:
                keep = rid >= s
                a_sh = jnp.where(keep, pltpu.roll(a, s, 0), 1.0)
                b_sh = jnp.where(keep, pltpu.roll(b, s, 0), 0.0)
            b = a * b_sh + b
            a = a * a_sh
        return a, b

    def scan_body(i, carry):
        hf, hb = carry
        rf = [pl.multiple_of((i * group + u) * 8, 8) for u in range(group)]
        rb = [pl.multiple_of(((n_steps - 1 - i) * group + (group - 1 - u)) * 8, 8) for u in range(group)]
        sf = [tile_scan(a_f[pl.ds(r, 8), :], b_f[pl.ds(r, 8), :], False) for r in rf]
        sb = [tile_scan(a_b[pl.ds(r, 8), :], b_b[pl.ds(r, 8), :], True) for r in rb]
        for u in range(group):
            h = sf[u][1] + sf[u][0] * hf
            b_f[pl.ds(rf[u], 8), :] = h
            hf = h[7:8]
            h = sb[u][1] + sb[u][0] * hb
            b_b[pl.ds(rb[u], 8), :] = h
            hb = h[0:1]
        return hf, hb

    h0 = h0_ref[0]
    hf, hb = lax.fori_loop(0, n_steps, scan_body, (h0[0:1], h0[1:2]))
    hfin_ref[0] = jnp.concatenate([hf, hb], axis=0)

    def out_body(i, c):
        r0 = pl.multiple_of(i * tile, tile)
        gl = gl_ref[0, pl.ds(r0, tile), :]
        hs = b_f[pl.ds(r0, tile), :] + b_b[pl.ds(r0, tile), :]
        out_ref[0, pl.ds(r0, tile), :] = (hs * jax.nn.gelu(gl)).astype(BF16)
        return c

    lax.fori_loop(0, n_tiles, out_body, 0)


def _lru(proj3, conv_w, conv_b, wg, bg, lam, h0):
    nb, seq, _ = proj3.shape
    tile = 256
    kern = functools.partial(_lru_kernel, seq=seq, tile=tile)
    return pl.pallas_call(
        kern,
        grid=(nb, LRU_HEADS),
        in_specs=[pl.BlockSpec((1, seq, LANES), lambda b, h: (b, 0, XL_BLK + h)),
                  pl.BlockSpec((1, seq, LANES), lambda b, h: (b, 0, GL_BLK + h)),
                  pl.BlockSpec((4, LANES), lambda b, h: (0, h)),
                  pl.BlockSpec((1, LANES), lambda b, h: (0, h)),
                  pl.BlockSpec((1, LANES, 512), lambda b, h: (h, 0, 0)),
                  pl.BlockSpec((1, 1, 512), lambda b, h: (h, 0, 0)),
                  pl.BlockSpec((1, 2, LANES), lambda b, h: (h, 0, 0)),
                  pl.BlockSpec((1, 2, LANES), lambda b, h: (b, 0, h))],
        out_specs=[pl.BlockSpec((1, seq, LANES), lambda b, h: (b, 0, h)),
                   pl.BlockSpec((1, 2, LANES), lambda b, h: (b, 0, h))],
        out_shape=[jax.ShapeDtypeStruct((nb, seq, LRU_WIDTH), BF16),
                   jax.ShapeDtypeStruct((nb, 2, LRU_WIDTH), F32)],
        scratch_shapes=[pltpu.VMEM((seq + 16, LANES), F32)] + [pltpu.VMEM((seq, LANES), F32)] * 4,
        compiler_params=_cparams(("parallel", "arbitrary")),
        name="lru",
    )(proj3, proj3, conv_w, conv_b, wg, bg, lam, h0)


def _shift_kinds(ch_lo, ch_hi, grid_mode):
    if grid_mode:
        q = RWKV_IN_WIDTH // 4
        parts = [("left", q), ("right", 2 * q), ("up", 3 * q), ("down", RWKV_IN_WIDTH)]
    else:
        parts = [("prev", RWKV_IN_WIDTH // 2), ("next", RWKV_IN_WIDTH)]
    kinds, lo = [], 0
    for kind, hi in parts:
        if ch_lo < hi and ch_hi > lo:
            kinds.append((kind, hi))
        lo = hi
    return kinds


def _shifted(x_ref, r0, seq, kind):
    C = CHUNK
    cur = x_ref[0, pl.ds(r0, C), :]
    rid = lax.broadcasted_iota(jnp.int32, cur.shape, 0)
    if kind == "left":
        return jnp.where(rid == 0, 0.0, pltpu.roll(cur, 1, 0))
    if kind == "right":
        return jnp.where(rid == C - 1, 0.0, pltpu.roll(cur, C - 1, 0))
    if kind == "up":
        src = pl.multiple_of(jnp.maximum(r0 - C, 0), C)
        return jnp.where(r0 > 0, x_ref[0, pl.ds(src, C), :], 0.0)
    if kind == "down":
        src = pl.multiple_of(jnp.minimum(r0 + C, seq - C), C)
        return jnp.where(r0 < seq - C, x_ref[0, pl.ds(src, C), :], 0.0)
    if kind == "prev":
        src = pl.multiple_of(jnp.maximum(r0 - 8, 0), 8)
        ext = jnp.concatenate([x_ref[0, pl.ds(src, 8), :], cur], axis=0)
        first = jnp.where(r0 == 0, 0, -1)
        return jnp.where(rid == first, 0.0, pltpu.roll(ext, 1, 0)[8:8 + C])
    assert kind == "next"
    src = pl.multiple_of(jnp.minimum(r0 + C, seq - 8), 8)
    ext = jnp.concatenate([cur, x_ref[0, pl.ds(src, 8), :]], axis=0)
    last = jnp.where(r0 == seq - C, C - 1, -1)
    return jnp.where(rid == last, 0.0, pltpu.roll(ext, C + 7, 0)[0:C])


def _token_shift(x_ref, r0, seq, mu, ch0, ch_lo, ch_hi, grid_mode):
    cur = x_ref[0, pl.ds(r0, CHUNK), :]
    kinds = _shift_kinds(ch_lo, ch_hi, grid_mode)
    sh = _shifted(x_ref, r0, seq, kinds[-1][0])
    if len(kinds) > 1:
        cg = ch0 + lax.broadcasted_iota(jnp.int32, cur.shape, 1)
        for kind, hi in reversed(kinds[:-1]):
            sh = jnp.where(cg < hi, _shifted(x_ref, r0, seq, kind), sh)
    return cur + mu * (sh - cur)


_V_W0, _V_A0, _V_KK, _V_KA, _V_RK, _V_LNW, _V_LNB = 0, 2, 4, 5, 6, 7, 8


def _wkv_kernel(*refs, seq, unroll, grid_mode, has_s0, want_fin):
    it = iter(refs)
    r_ref, k_ref, v_ref, lo_ref, mu_ref, mul_ref, ww_ref, wa_ref, wgt_ref, vec_ref = (next(it) for _ in range(10))
    s0_ref = next(it) if has_s0 else None
    out_ref = next(it)
    sfin_ref = next(it) if want_fin else None
    q_s, lr_s, g_s, gam_s, ybuf, loa_s = it
    n_chunks = seq // CHUNK
    C = CHUNK
    pair = pl.program_id(1)
    vec = vec_ref[0]
    mu = mu_ref[0]

    @pl.when(pair == 0)
    def _():
        def lora_body(i, carry):
            r0 = pl.multiple_of(i * C, C)
            y = _token_shift(lo_ref, r0, seq, mul_ref[...], 3 * RWKV_WIDTH, 3 * RWKV_WIDTH, RWKV_IN_WIDTH, grid_mode)
            ln = lax.broadcasted_iota(jnp.int32, y.shape, 1)
            act = jnp.where(ln < DECAY_LORA, jnp.tanh(y),
                            jnp.where(ln < DECAY_LORA + AAA_LORA, y, jax.nn.sigmoid(y)))
            loa_s[pl.ds(r0, C), :] = act.astype(BF16)
            return carry

        lax.fori_loop(0, n_chunks, lora_body, 0)

    lane = lax.broadcasted_iota(jnp.int32, (C, LANES), 1)
    head0 = lane < HEAD_SIZE
    ri = lax.broadcasted_iota(jnp.int32, (LANES, LANES), 0)
    ci = lax.broadcasted_iota(jnp.int32, (LANES, LANES), 1)
    seg_ones = ((ri // HEAD_SIZE) == (ci // HEAD_SIZE)).astype(BF16)
    eye = (ri == ci).astype(F32)
    tr = lax.broadcasted_iota(jnp.int32, (C, C), 0)
    tc = lax.broadcasted_iota(jnp.int32, (C, C), 1)
    tri = ((tc <= tr).astype(BF16), (tc >= tr).astype(BF16))
    br = lax.broadcasted_iota(jnp.int32, (2 * LANES, 2 * LANES), 0)
    bc = lax.broadcasted_iota(jnp.int32, (2 * LANES, 2 * LANES), 1)
    brt, bct = br % C, bc % C
    incl = (br >= LANES).astype(jnp.int32)
    big_mask = (bct < brt + incl, bct > brt - incl)

    def stack(x):
        return jnp.concatenate([jnp.where(head0, x, 0.0), jnp.where(head0, 0.0, x)], axis=0).astype(BF16)

    def each(fn, *lists):
        return [fn(*xs) for xs in zip(*lists)]

    def bf(xs):
        return [x.astype(BF16) for x in xs]

    def prep_group(chunks):
        rows = [pl.multiple_of(c * C, C) for c in chunks]
        ch0 = pair * LANES
        r = [_token_shift(r_ref, r0, seq, mu[0:1], ch0, 0, RWKV_WIDTH, grid_mode) for r0 in rows]
        k = [_token_shift(k_ref, r0, seq, mu[1:2], RWKV_WIDTH + ch0, RWKV_WIDTH, 2 * RWKV_WIDTH, grid_mode)
             for r0 in rows]
        v = [_token_shift(v_ref, r0, seq, mu[2:3], 2 * RWKV_WIDTH + ch0, 2 * RWKV_WIDTH, 3 * RWKV_WIDTH, grid_mode)
             for r0 in rows]
        lo = [loa_s[pl.ds(r0, C), 0:LANES] for r0 in rows]
        lw = each(lambda x: _dot(x, ww_ref[0]), lo)
        la = each(lambda x: _dot(x, wa_ref[0]), lo)
        kk = each(lambda x: x * vec[_V_KK:_V_KK + 1], k)
        ss = each(lambda x: _dot_sel_r(x * x, seg_ones), kk)
        kk = each(lambda x, s: x * lax.rsqrt(jnp.maximum(s, 1e-24)), kk, ss)
        bonus = each(lambda rr, kx, vx: _dot_sel_r(rr * kx * vec[_V_RK:_V_RK + 1], seg_ones) * vx, r, k, v)
        vs = each(stack, v)
        ch = [(i, d) for i in range(len(chunks)) for d in range(2)]
        logw = [-jnp.exp(-_softplus(-(vec[_V_W0 + d:_V_W0 + d + 1] + lw[i][:, d * LANES:(d + 1) * LANES])) - 0.5)
                for i, d in ch]
        a = [jax.nn.sigmoid(vec[_V_A0 + d:_V_A0 + d + 1] + la[i][:, d * LANES:(d + 1) * LANES]) for i, d in ch]
        kd = [k[i] * (1.0 + (ax - 1.0) * vec[_V_KA:_V_KA + 1]) for (i, d), ax in zip(ch, a)]
        bt = [kk[i] * ax for (i, d), ax in zip(ch, a)]
        cl = [_dot_sel_l(tri[d], lx) for (i, d), lx in zip(ch, logw)]
        ltot = each(lambda lx: jnp.sum(lx, axis=0, keepdims=True), logw)
        e_neg = each(lambda x: jnp.exp(-x), cl)
        e_tot = each(lambda t, x: jnp.exp(t - x), ltot, cl)
        ar = [jnp.concatenate([stack(-kk[i] * jnp.exp(cx - lx)), stack(r[i] * jnp.exp(cx))], axis=0)
              for (i, d), cx, lx in zip(ch, cl, logw)]
        bk = each(lambda b, kx, e: jnp.concatenate([stack(b * e), stack(kx * e)], axis=0), bt, kd, e_neg)
        bkc = each(lambda b, kx, e: jnp.concatenate([stack(b * e), stack(kx * e)], axis=0), bt, kd, e_tot)
        big = [jnp.where(big_mask[d], _dot_nt(x, y), 0.0) for (i, d), x, y in zip(ch, ar, bk)]
        n1 = [b[0:LANES, 0:LANES] for b in big]
        n1b = bf(n1)
        xyv = [_dot(b[:, LANES:].astype(BF16), vs[i]) for (i, d), b in zip(ch, big)]
        n2 = each(_dot, n1b, n1b)
        n2b = bf(n2)
        n4 = each(_dot, n2b, n2b)
        p1 = each(lambda x1, x2, x1b, x2b: eye + x1 + x2 + _dot(x1b, x2b), n1, n2, n1b, n2b)
        n4b = bf(n4)
        n8 = each(_dot, n4b, n4b)
        n8b = bf(n8)
        n16 = each(_dot, n8b, n8b)
        p2 = each(lambda x1, x2, x1b, x2b: eye + x1 + x2 + _dot(x1b, x2b), n4, n8, n4b, n8b)
        n16b = bf(n16)
        n32 = each(_dot, n16b, n16b)
        p12 = each(_dot, bf(p1), bf(p2))
        p3 = each(lambda x1, x2, x1b, x2b: eye + x1 + x2 + _dot(x1b, x2b), n16, n32, n16b, bf(n32))
        tb = bf(each(_dot, bf(p12), bf(p3)))
        aub = bf(each(lambda t, x, y: _dot(t, jnp.concatenate([x[0:LANES], y[0:LANES].astype(BF16)], axis=1)),
                      tb, ar, xyv))
        qy = each(lambda b, x: _dot(b[LANES:, 0:LANES].astype(BF16), x), big, aub)
        lr = each(lambda x, y: _dot_tn(x[:, 0:LANES], y[0:LANES]), aub, bkc)
        g = [_dot_tn(jnp.concatenate([x[:, LANES:], vs[i]], axis=0), y) for (i, d), x, y in zip(ch, aub, bkc)]
        for (i, d), x, q, l, gx, t in zip(ch, ar, qy, lr, g, ltot):
            c = chunks[i]
            q_s[d, c] = (x[LANES:].astype(F32) + q[:, 0:LANES]).astype(BF16)
            lr_s[d, c] = l.astype(BF16)
            g_s[d, c] = gx
            gam_s[d, c] = jnp.broadcast_to(jnp.exp(t), (8, LANES))
        y0 = each(lambda x, q: x[LANES:] + q[:, LANES:], xyv, qy)
        for i, r0 in enumerate(rows):
            y = bonus[i]
            for d in range(2):
                yd = y0[2 * i + d]
                y = y + yd[0:C] + yd[C:]
            ybuf[pl.ds(r0, C), :] = y

    def prep_body(i, carry):
        prep_group([i * unroll + u for u in range(unroll)])
        return carry

    lax.fori_loop(0, n_chunks // unroll, prep_body, 0)

    def step_body(i, carry):
        cs = (i, n_chunks - 1 - i)
        sb = bf(carry)
        ys = [_dot_nt(q_s[d, cs[d]], sb[d]) for d in range(2)]
        ls = [_dot(sb[d], lr_s[d, cs[d]]) for d in range(2)]
        for d in range(2):
            r0 = pl.multiple_of(cs[d] * C, C)
            ybuf[pl.ds(r0, C), :] += ys[d][0:C] + ys[d][C:]
        return tuple(carry[d] * gam_s[d, cs[d]][0:1] + ls[d] + g_s[d, cs[d]] for d in range(2))

    zero_h = jnp.zeros((HEAD_SIZE, HEAD_SIZE), F32)

    def pair_state(s2):
        return jnp.concatenate([jnp.concatenate([s2[0], zero_h], axis=1),
                                jnp.concatenate([zero_h, s2[1]], axis=1)], axis=0)

    if has_s0:
        init = (pair_state(s0_ref[0, 0]), pair_state(s0_ref[0, 1]))
    else:
        init = (jnp.zeros((LANES, LANES), F32),) * 2
    fin = lax.fori_loop(0, n_chunks, step_body, init)
    if want_fin:
        for d in range(2):
            sfin_ref[0, d, 0] = fin[d][0:HEAD_SIZE, 0:HEAD_SIZE]
            sfin_ref[0, d, 1] = fin[d][HEAD_SIZE:, HEAD_SIZE:]

    def out_body(i, carry):
        rows = [pl.multiple_of((i * unroll + u) * C, C) for u in range(unroll)]
        y = [ybuf[pl.ds(r0, C), :] for r0 in rows]
        g = [_dot(loa_s[pl.ds(r0, C), LANES:LORA_OUT], wgt_ref[0]) for r0 in rows]
        mean = each(lambda x: _dot_sel_r(x, seg_ones) * (1.0 / HEAD_SIZE), y)
        dlt = each(lambda x, m: x - m, y, mean)
        var = each(lambda x: _dot_sel_r(x * x, seg_ones) * (1.0 / HEAD_SIZE), dlt)
        for r0, dx, vx, gx in zip(rows, dlt, var, g):
            yn = dx * lax.rsqrt(vx + GN_EPS) * vec[_V_LNW:_V_LNW + 1] + vec[_V_LNB:_V_LNB + 1]
            out_ref[0, pl.ds(r0, C), :] = (yn * gx).astype(BF16)
        return carry

    lax.fori_loop(0, n_chunks // unroll, out_body, 0)


def _wkv(proj3, mu3, mu_lora, ww, wa, wgt, vec, s0, grid_mode, want_fin):
    nb, seq, _ = proj3.shape
    n_chunks = seq // CHUNK
    has_s0 = s0 is not None
    kern = functools.partial(_wkv_kernel, seq=seq, unroll=4, grid_mode=grid_mode, has_s0=has_s0, want_fin=want_fin)
    state_spec = pl.BlockSpec((1, 2, 2, HEAD_SIZE, HEAD_SIZE), lambda b, j: (b, 0, j, 0, 0))
    in_specs = [pl.BlockSpec((1, seq, LANES), lambda b, j: (b, 0, R_BLK + j)),
                pl.BlockSpec((1, seq, LANES), lambda b, j: (b, 0, K_BLK + j)),
                pl.BlockSpec((1, seq, LANES), lambda b, j: (b, 0, V_BLK + j)),
                pl.BlockSpec((1, seq, LORA_OUT), lambda b, j: (b, 0, 0)),
                pl.BlockSpec((1, 3, LANES), lambda b, j: (j, 0, 0)),
                pl.BlockSpec((1, LORA_OUT), lambda b, j: (0, 0)),
                pl.BlockSpec((1, LANES, 2 * LANES), lambda b, j: (j, 0, 0)),
                pl.BlockSpec((1, LANES, 2 * LANES), lambda b, j: (j, 0, 0)),
                pl.BlockSpec((1, 2 * LANES, LANES), lambda b, j: (j, 0, 0)),
                pl.BlockSpec((1, 16, LANES), lambda b, j: (j, 0, 0))]
    args = [proj3, proj3, proj3, proj3, mu3, mu_lora, ww, wa, wgt, vec]
    out_specs = [pl.BlockSpec((1, seq, LANES), lambda b, j: (b, 0, j))]
    out_shape = [jax.ShapeDtypeStruct((nb, seq, RWKV_WIDTH), BF16)]
    if has_s0:
        in_specs.append(state_spec)
        args.append(s0)
    if want_fin:
        out_specs.append(state_spec)
        out_shape.append(jax.ShapeDtypeStruct((nb, 2, RWKV_HEADS, HEAD_SIZE, HEAD_SIZE), F32))
    res = pl.pallas_call(
        kern,
        grid=(nb, PAIRS),
        in_specs=in_specs,
        out_specs=out_specs,
        out_shape=out_shape,
        scratch_shapes=[pltpu.VMEM((2, n_chunks, LANES, LANES), BF16),
                        pltpu.VMEM((2, n_chunks, LANES, LANES), BF16),
                        pltpu.VMEM((2, n_chunks, LANES, LANES), F32),
                        pltpu.VMEM((2, n_chunks, 8, LANES), F32),
                        pltpu.VMEM((seq, LANES), F32),
                        pltpu.VMEM((seq, LORA_OUT), BF16)],
        compiler_params=_cparams(("arbitrary", "arbitrary")),
        name="wkv",
    )(*args)
    return res[0], (res[1] if want_fin else None)


def _out_proj_kernel(lru_ref, rw_ref, w1_ref, w2_ref, x_ref, mod_ref, g_ref, x1_ref, h2_ref):
    out = _dot(lru_ref[...], w1_ref[...]) + _dot(rw_ref[...], w2_ref[...])
    m = mod_ref[0]
    g = g_ref[...]
    x1 = x_ref[...] + m[2:3] * _rms(out, g[0:1])
    x1_ref[...] = x1
    h2_ref[...] = (_rms(x1, g[1:2]) * (1.0 + m[4:5]) + m[3:4]).astype(BF16)


def _out_proj(lru_o, rw_o, w_out_b, x2d, mod3, g2, mod_idx, tm=512):
    m = x2d.shape[0]
    half = D_MODEL // 2
    return pl.pallas_call(
        _out_proj_kernel,
        grid=(m // tm,),
        in_specs=[pl.BlockSpec((tm, half), lambda i: (i, 0)),
                  pl.BlockSpec((tm, half), lambda i: (i, 0)),
                  pl.BlockSpec((half, D_MODEL), lambda i: (0, 0)),
                  pl.BlockSpec((half, D_MODEL), lambda i: (1, 0)),
                  pl.BlockSpec((tm, D_MODEL), lambda i: (i, 0)),
                  pl.BlockSpec((1, 6, D_MODEL), lambda i: (mod_idx(i * tm), 0, 0)),
                  pl.BlockSpec((2, D_MODEL), lambda i: (0, 0))],
        out_specs=[pl.BlockSpec((tm, D_MODEL), lambda i: (i, 0)),
                   pl.BlockSpec((tm, D_MODEL), lambda i: (i, 0))],
        out_shape=[jax.ShapeDtypeStruct((m, D_MODEL), F32),
                   jax.ShapeDtypeStruct((m, D_MODEL), BF16)],
        compiler_params=_cparams(("parallel",)),
        name="out_proj",
    )(lru_o, rw_o, w_out_b, w_out_b, x2d, mod3, g2)


def _ffn_kernel(h_ref, wg_ref, wu_ref, wd_ref, x1_ref, mod_ref, g_ref, o_ref, acc):
    f = pl.program_id(1)
    h = h_ref[...]
    gate = _dot(h, wg_ref[...])
    up = _dot(h, wu_ref[...])
    act = (gate * jax.nn.sigmoid(gate) * up).astype(BF16)
    part = _dot(act, wd_ref[...])

    @pl.when(f == 0)
    def _():
        acc[...] = part

    @pl.when(f > 0)
    def _():
        acc[...] += part

    @pl.when(f == pl.num_programs(1) - 1)
    def _():
        o_ref[...] = x1_ref[...] + mod_ref[0][5:6] * _rms(acc[...], g_ref[...])


def _ffn(h2, w_gu_b, w_down_b, x1, mod3, g, mod_idx, tm=512, tf=512):
    m = h2.shape[0]
    nf = FFN_HIDDEN // tf
    return pl.pallas_call(
        _ffn_kernel,
        grid=(m // tm, nf),
        in_specs=[pl.BlockSpec((tm, D_MODEL), lambda i, f: (i, 0)),
                  pl.BlockSpec((D_MODEL, tf), lambda i, f: (0, f)),
                  pl.BlockSpec((D_MODEL, tf), lambda i, f: (0, nf + f)),
                  pl.BlockSpec((tf, D_MODEL), lambda i, f: (f, 0)),
                  pl.BlockSpec((tm, D_MODEL), lambda i, f: (i, 0)),
                  pl.BlockSpec((1, 6, D_MODEL), lambda i, f: (mod_idx(i * tm), 0, 0)),
                  pl.BlockSpec((1, D_MODEL), lambda i, f: (0, 0))],
        out_specs=pl.BlockSpec((tm, D_MODEL), lambda i, f: (i, 0)),
        out_shape=jax.ShapeDtypeStruct((m, D_MODEL), F32),
        scratch_shapes=[pltpu.VMEM((tm, D_MODEL), F32)],
        compiler_params=_cparams(("parallel", "arbitrary")),
        name="ffn",
    )(h2, w_gu_b, w_gu_b, w_down_b, x1, mod3, g)


def _group(x3, mod3, mod_idx, grid_mode, lru_h0, wkv_s0, want_fin, P):
    nb, seq, _ = x3.shape
    x2d = x3.reshape(nb * seq, D_MODEL)
    proj = _in_proj(x2d, mod3, P["norm_mix_pre"], P["w_in"], mod_idx).reshape(nb, seq, PROJ_PAD)
    lru_o, lru_fin = _lru(proj, P["conv_w"], P["conv_b"], P["lru_wg"], P["lru_bg"], P["lru_lam"], lru_h0)
    rw_o, s_fin = _wkv(proj, P["mu3"], P["mu_lora"], P["ww"], P["wa"], P["wgt"], P["vec"], wkv_s0,
                       grid_mode, want_fin)
    x1, h2 = _out_proj(lru_o.reshape(nb * seq, LRU_WIDTH), rw_o.reshape(nb * seq, RWKV_WIDTH),
                       P["w_out"], x2d, mod3, P["g_post_pre"], mod_idx)
    y = _ffn(h2, P["w_gu"], P["w_down"], x1, mod3, P["norm_ffn_post"], mod_idx)
    return y.reshape(nb, seq, D_MODEL), lru_fin, s_fin


def _prep_params(norm_mix_pre, norm_mix_post, norm_ffn_pre, norm_ffn_post, w_in,
                 lru_conv_w, lru_conv_b, lru_wr, lru_br, lru_wi, lru_bi, lru_lambda,
                 rwkv_mu, rwkv_w0, rwkv_w_up, rwkv_a0, rwkv_a_up, rwkv_g_up, rwkv_k_k, rwkv_k_a, rwkv_r_k,
                 rwkv_ln_w, rwkv_ln_b, w_out, ffn_w_gu, ffn_w_down):
    P = {}
    P["norm_mix_pre"] = norm_mix_pre.reshape(1, D_MODEL)
    P["g_post_pre"] = jnp.stack([norm_mix_post, norm_ffn_pre], axis=0)
    P["norm_ffn_post"] = norm_ffn_post.reshape(1, D_MODEL)
    w_in_b = w_in.astype(BF16)
    n_main = IN_WIDTH - LORA_WIDTH
    P["w_in"] = jnp.concatenate([w_in_b[:, n_main:], jnp.zeros((D_MODEL, LORA_PAD - LORA_WIDTH), BF16),
                                 w_in_b[:, :n_main]], axis=1)
    P["conv_w"] = lru_conv_w
    P["conv_b"] = lru_conv_b.reshape(1, LRU_WIDTH)
    P["lru_wg"] = jnp.concatenate([lru_wr[0], lru_wi[0], lru_wr[1], lru_wi[1]], axis=-1).astype(BF16)
    P["lru_bg"] = jnp.concatenate([lru_br[0], lru_bi[0], lru_br[1], lru_bi[1]], axis=-1).reshape(LRU_HEADS, 1, 512)
    P["lru_lam"] = jnp.swapaxes(lru_lambda.reshape(2, LRU_HEADS, LRU_HEAD_DIM), 0, 1)
    P["mu3"] = jnp.swapaxes(rwkv_mu[:3 * RWKV_WIDTH].reshape(3, PAIRS, LANES), 0, 1)
    P["mu_lora"] = jnp.pad(rwkv_mu[3 * RWKV_WIDTH:], (0, LORA_OUT - LORA_WIDTH)).reshape(1, LORA_OUT)

    def per_pair(w):
        return jnp.swapaxes(w.reshape(w.shape[0], PAIRS, LANES), 0, 1)

    wu = jnp.concatenate([per_pair(rwkv_w_up[0]), per_pair(rwkv_w_up[1])], axis=-1)
    au = jnp.concatenate([per_pair(rwkv_a_up[0]), per_pair(rwkv_a_up[1])], axis=-1)
    P["ww"] = jnp.pad(wu, ((0, 0), (0, LANES - DECAY_LORA), (0, 0))).astype(BF16)
    P["wa"] = jnp.pad(au, ((0, 0), (DECAY_LORA, LANES - DECAY_LORA - AAA_LORA), (0, 0))).astype(BF16)
    P["wgt"] = jnp.pad(per_pair(rwkv_g_up), ((0, 0), (0, 2 * LANES - GATE_LORA), (0, 0))).astype(BF16)
    rows = [rwkv_w0[0], rwkv_w0[1], rwkv_a0[0], rwkv_a0[1], rwkv_k_k, rwkv_k_a,
            rwkv_r_k.reshape(RWKV_WIDTH), rwkv_ln_w, rwkv_ln_b]
    vec = jnp.stack(rows + [jnp.zeros_like(rwkv_k_k)] * (16 - len(rows)), axis=0)
    P["vec"] = per_pair(vec)
    P["w_out"] = w_out.astype(BF16)
    P["w_gu"] = ffn_w_gu.astype(BF16)
    P["w_down"] = ffn_w_down.astype(BF16)
    return P


def kernel(x_prompt, x_sample, state_lru, state_wkv, c, c_ctx, norm_mix_pre, norm_mix_post, norm_ffn_pre,
           norm_ffn_post, w_mod, b_mod, w_in, lru_conv_w, lru_conv_b, lru_wr, lru_br, lru_wi, lru_bi,
           lru_lambda, rwkv_mu, rwkv_w0, rwkv_w_up, rwkv_a0, rwkv_a_up, rwkv_g_up, rwkv_k_k, rwkv_k_a,
           rwkv_r_k, rwkv_ln_w, rwkv_ln_b, w_out, ffn_w_gu, ffn_w_down):
    depth = w_in.shape[0]
    nb_p = x_prompt.shape[0]
    nb_s, seq_s, _ = x_sample.shape
    y_p, y_s = x_prompt, x_sample
    new_lru, new_wkv = [], []
    cvec = jnp.concatenate([c_ctx[None, :], c, jnp.zeros((8 - 1 - nb_s, D_MODEL), F32)], axis=0)
    for l in range(depth):
        P = _prep_params(norm_mix_pre[l], norm_mix_post[l], norm_ffn_pre[l], norm_ffn_post[l], w_in[l],
                         lru_conv_w[l], lru_conv_b[l], lru_wr[l], lru_br[l], lru_wi[l], lru_bi[l],
                         lru_lambda[l], rwkv_mu[l], rwkv_w0[l], rwkv_w_up[l], rwkv_a0[l], rwkv_a_up[l],
                         rwkv_g_up[l], rwkv_k_k[l], rwkv_k_a[l], rwkv_r_k[l], rwkv_ln_w[l], rwkv_ln_b[l],
                         w_out[l], ffn_w_gu[l], ffn_w_down[l])
        mod3 = _mod(cvec, w_mod[l], b_mod[l].reshape(1, -1)).reshape(8, 6, D_MODEL)
        y_p, lru_ctx, wkv_ctx = _group(
            y_p, mod3, lambda row: 0, False, jnp.zeros((nb_p, 2, LRU_WIDTH), F32), None, True, P)
        new_lru.append(lru_ctx)
        new_wkv.append(wkv_ctx)
        y_s, _, _ = _group(
            y_s, mod3, lambda row: 1 + row // seq_s, True, state_lru[:, l], state_wkv[:, l], False, P)
    return (y_p, y_s, jnp.stack(new_lru, axis=1), jnp.stack(new_wkv, axis=1))
```

```python
import functools

import jax
import jax.numpy as jnp
from jax import lax
from jax.experimental import pallas as pl
from jax.experimental.pallas import tpu as pltpu

F32 = jnp.float32
BF16 = jnp.bfloat16

D_MODEL = 2048
LRU_WIDTH = 1024
LRU_HEADS = 8
LRU_HEAD_DIM = 128
LRU_C = 8.0
RWKV_WIDTH = 1024
HEAD_SIZE = 64
RWKV_HEADS = 16
DECAY_LORA = 64
AAA_LORA = 64
GATE_LORA = 160
LORA_WIDTH = DECAY_LORA + AAA_LORA + GATE_LORA
RWKV_IN_WIDTH = 3 * RWKV_WIDTH + LORA_WIDTH
IN_WIDTH = 2 * LRU_WIDTH + RWKV_IN_WIDTH
FFN_HIDDEN = 5632
GRID_W = 64
RMS_EPS = 1e-6
GN_EPS = 64e-5

LANES = 128
LORA_OUT = 384
LORA_PAD = 512
PROJ_PAD = LORA_PAD + 2 * LRU_WIDTH + 3 * RWKV_WIDTH
XL_BLK = LORA_PAD // 128
GL_BLK = XL_BLK + LRU_WIDTH // 128
R_BLK = GL_BLK + LRU_WIDTH // 128
K_BLK = R_BLK + RWKV_WIDTH // 128
V_BLK = K_BLK + RWKV_WIDTH // 128
CHUNK = 64
PAIRS = RWKV_HEADS // 2
VMEM_LIMIT = 56 * 1024 * 1024


def _cparams(sem, flags=None):
    return pltpu.CompilerParams(dimension_semantics=sem, vmem_limit_bytes=VMEM_LIMIT, flags=flags)


def _dot(a, b):
    return jnp.dot(a, b, preferred_element_type=F32)


def _dot_nt(a, b):
    return lax.dot_general(a, b, (((1,), (1,)), ((), ())), preferred_element_type=F32)


def _dot_tn(a, b):
    return lax.dot_general(a, b, (((0,), (0,)), ((), ())), preferred_element_type=F32)


def _split3(x):
    hi = x.astype(BF16)
    r1 = x - hi.astype(F32)
    mid = r1.astype(BF16)
    lo = (r1 - mid.astype(F32)).astype(BF16)
    return hi, mid, lo


def _softplus(x):
    return jnp.maximum(x, 0.0) + jnp.log1p(jnp.exp(-jnp.abs(x)))


def _rms(x, g):
    return x * lax.rsqrt(jnp.mean(x * x, axis=-1, keepdims=True) + RMS_EPS) * g


def _mod_kernel(c_ref, w_ref, b_ref, o_ref):
    c = c_ref[...]
    s = c * jax.nn.sigmoid(c)
    o_ref[...] = jnp.dot(s, w_ref[...], precision=lax.Precision.HIGHEST,
                         preferred_element_type=F32) + b_ref[...]


def _mod(cvec, w_mod, b_mod):
    n = w_mod.shape[1]
    tn = 1024
    return pl.pallas_call(
        _mod_kernel,
        grid=(n // tn,),
        in_specs=[pl.BlockSpec((8, D_MODEL), lambda j: (0, 0)),
                  pl.BlockSpec((D_MODEL, tn), lambda j: (0, j)),
                  pl.BlockSpec((1, tn), lambda j: (0, j))],
        out_specs=pl.BlockSpec((8, tn), lambda j: (0, j)),
        out_shape=jax.ShapeDtypeStruct((8, n), F32),
        compiler_params=_cparams(("parallel",)),
        name="mod",
    )(cvec, w_mod, b_mod)


def _in_proj_kernel(x_ref, mod_ref, g_ref, w_ref, o_ref, h_scr):
    @pl.when(pl.program_id(1) == 0)
    def _():
        m = mod_ref[0]
        h = _rms(x_ref[...], g_ref[...]) * (1.0 + m[1:2]) + m[0:1]
        h_scr[...] = h.astype(BF16)

    o_ref[...] = _dot(h_scr[...], w_ref[...])


def _in_proj(x2d, mod3, g, w_in_p, mod_idx, tm=1024, tn=1408):
    m = x2d.shape[0]
    return pl.pallas_call(
        _in_proj_kernel,
        grid=(m // tm, PROJ_PAD // tn),
        in_specs=[pl.BlockSpec((tm, D_MODEL), lambda i, j: (i, 0)),
                  pl.BlockSpec((1, 6, D_MODEL), lambda i, j: (mod_idx(i * tm), 0, 0)),
                  pl.BlockSpec((1, D_MODEL), lambda i, j: (0, 0)),
                  pl.BlockSpec((D_MODEL, tn), lambda i, j: (0, j))],
        out_specs=pl.BlockSpec((tm, tn), lambda i, j: (i, j)),
        out_shape=jax.ShapeDtypeStruct((m, PROJ_PAD), F32),
        scratch_shapes=[pltpu.VMEM((tm, D_MODEL), BF16)],
        compiler_params=_cparams(("parallel", "arbitrary")),
        name="in_proj",
    )(x2d, mod3, g, w_in_p)


def _lru_kernel(xl_ref, gl_ref, cw_ref, cb_ref, wg_ref, bg_ref, lam_ref, h0_ref,
                out_ref, hfin_ref, xpad, a_f, b_f, a_b, b_b, *, seq, tile):
    n_tiles = seq // tile
    zeros8 = jnp.zeros((8, LANES), F32)
    xpad[pl.ds(0, 8), :] = zeros8
    xpad[pl.ds(seq + 8, 8), :] = zeros8

    def copy_body(i, c):
        r0 = pl.multiple_of(i * tile, tile)
        xpad[pl.ds(r0 + 8, tile), :] = xl_ref[0, pl.ds(r0, tile), :]
        return c

    lax.fori_loop(0, n_tiles, copy_body, 0)

    cneg = -LRU_C * _softplus(-lam_ref[0])
    cw = cw_ref[...]
    cb = cb_ref[...]
    bg = bg_ref[0]

    def gate_body(i, c):
        r0 = pl.multiple_of(i * tile, tile)
        ext = xpad[pl.ds(r0, tile + 16), :]
        n_ext = tile + 16
        xc = (pltpu.roll(ext, 2, 0)[8:8 + tile] * cw[0:1]
              + pltpu.roll(ext, 1, 0)[8:8 + tile] * cw[1:2]
              + ext[8:8 + tile] * cw[2:3]
              + pltpu.roll(ext, n_ext - 1, 0)[8:8 + tile] * cw[3:4]) + cb
        g = _dot(xc.astype(BF16), wg_ref[0]) + bg
        for d, (a_s, b_s) in enumerate(((a_f, b_f), (a_b, b_b))):
            r = jax.nn.sigmoid(g[:, 256 * d:256 * d + 128])
            ig = jax.nn.sigmoid(g[:, 256 * d + 128:256 * d + 256])
            log_a = cneg[d:d + 1] * r
            a = jnp.exp(log_a)
            one_m_a2 = -jnp.tanh(log_a) * (a * a + 1.0)
            a_s[pl.ds(r0, tile), :] = a
            b_s[pl.ds(r0, tile), :] = jnp.sqrt(one_m_a2) * (ig * xc)
        return c

    lax.fori_loop(0, n_tiles, gate_body, 0)

    group = 4
    n_steps = seq // (8 * group)
    rid = lax.broadcasted_iota(jnp.int32, (8, LANES), 0)

    def tile_scan(a, b, reverse):
        for s in (1, 2, 4):
            if reverse:
                keep = rid < 8 - s
                a_sh = jnp.where(keep, pltpu.roll(a, 8 - s, 0), 1.0)
                b_sh = jnp.where(keep, pltpu.roll(b, 8 - s, 0), 0.0)
            else:
                keep = rid >= s
                a_sh = jnp.where(keep, pltpu.roll(a, s, 0), 1.0)
                b_sh = jnp.where(keep, pltpu.roll(b, s, 0), 0.0)
            b = a * b_sh + b
            a = a * a_sh
        return a, b

    def scan_body(i, carry):
        hf, hb = carry
        rf = [pl.multiple_of((i * group + u) * 8, 8) for u in range(group)]
        rb = [pl.multiple_of(((n_steps - 1 - i) * group + (group - 1 - u)) * 8, 8) for u in range(group)]
        sf = [tile_scan(a_f[pl.ds(r, 8), :], b_f[pl.ds(r, 8), :], False) for r in rf]
        sb = [tile_scan(a_b[pl.ds(r, 8), :], b_b[pl.ds(r, 8), :], True) for r in rb]
        for u in range(group):
            h = sf[u][1] + sf[u][0] * hf
            b_f[pl.ds(rf[u], 8), :] = h
            hf = h[7:8]
            h = sb[u][1] + sb[u][0] * hb
            b_b[pl.ds(rb[u], 8), :] = h
            hb = h[0:1]
        return hf, hb

    h0 = h0_ref[0]
    hf, hb = lax.fori_loop(0, n_steps, scan_body, (h0[0:1], h0[1:2]))
    hfin_ref[0] = jnp.concatenate([hf, hb], axis=0)

    def out_body(i, c):
        r0 = pl.multiple_of(i * tile, tile)
        gl = gl_ref[0, pl.ds(r0, tile), :]
        hs = b_f[pl.ds(r0, tile), :] + b_b[pl.ds(r0, tile), :]
        out_ref[0, pl.ds(r0, tile), :] = (hs * jax.nn.gelu(gl)).astype(BF16)
        return c

    lax.fori_loop(0, n_tiles, out_body, 0)


def _lru(proj3, conv_w, conv_b, wg, bg, lam, h0):
    nb, seq, _ = proj3.shape
    tile = 256
    kern = functools.partial(_lru_kernel, seq=seq, tile=tile)
    return pl.pallas_call(
        kern,
        grid=(nb, LRU_HEADS),
        in_specs=[pl.BlockSpec((1, seq, LANES), lambda b, h: (b, 0, XL_BLK + h)),
                  pl.BlockSpec((1, seq, LANES), lambda b, h: (b, 0, GL_BLK + h)),
                  pl.BlockSpec((4, LANES), lambda b, h: (0, h)),
                  pl.BlockSpec((1, LANES), lambda b, h: (0, h)),
                  pl.BlockSpec((1, LANES, 512), lambda b, h: (h, 0, 0)),
                  pl.BlockSpec((1, 1, 512), lambda b, h: (h, 0, 0)),
                  pl.BlockSpec((1, 2, LANES), lambda b, h: (h, 0, 0)),
                  pl.BlockSpec((1, 2, LANES), lambda b, h: (b, 0, h))],
        out_specs=[pl.BlockSpec((1, seq, LANES), lambda b, h: (b, 0, h)),
                   pl.BlockSpec((1, 2, LANES), lambda b, h: (b, 0, h))],
        out_shape=[jax.ShapeDtypeStruct((nb, seq, LRU_WIDTH), BF16),
                   jax.ShapeDtypeStruct((nb, 2, LRU_WIDTH), F32)],
        scratch_shapes=[pltpu.VMEM((seq + 16, LANES), F32)] + [pltpu.VMEM((seq, LANES), F32)] * 4,
        compiler_params=_cparams(("parallel", "arbitrary")),
        name="lru",
    )(proj3, proj3, conv_w, conv_b, wg, bg, lam, h0)


def _shift_kinds(ch_lo, ch_hi, grid_mode):
    if grid_mode:
        q = RWKV_IN_WIDTH // 4
        parts = [("left", q), ("right", 2 * q), ("up", 3 * q), ("down", RWKV_IN_WIDTH)]
    else:
        parts = [("prev", RWKV_IN_WIDTH // 2), ("next", RWKV_IN_WIDTH)]
    kinds, lo = [], 0
    for kind, hi in parts:
        if ch_lo < hi and ch_hi > lo:
            kinds.append((kind, hi))
        lo = hi
    return kinds


def _shifted(x_ref, r0, seq, kind):
    C = CHUNK
    cur = x_ref[0, pl.ds(r0, C), :]
    rid = lax.broadcasted_iota(jnp.int32, cur.shape, 0)
    if kind == "left":
        return jnp.where(rid == 0, 0.0, pltpu.roll(cur, 1, 0))
    if kind == "right":
        return jnp.where(rid == C - 1, 0.0, pltpu.roll(cur, C - 1, 0))
    if kind == "up":
        src = pl.multiple_of(jnp.maximum(r0 - C, 0), C)
        return jnp.where(r0 > 0, x_ref[0, pl.ds(src, C), :], 0.0)
    if kind == "down":
        src = pl.multiple_of(jnp.minimum(r0 + C, seq - C), C)
        return jnp.where(r0 < seq - C, x_ref[0, pl.ds(src, C), :], 0.0)
    if kind == "prev":
        src = pl.multiple_of(jnp.maximum(r0 - 8, 0), 8)
        ext = jnp.concatenate([x_ref[0, pl.ds(src, 8), :], cur], axis=0)
        first = jnp.where(r0 == 0, 0, -1)
        return jnp.where(rid == first, 0.0, pltpu.roll(ext, 1, 0)[8:8 + C])
    assert kind == "next"
    src = pl.multiple_of(jnp.minimum(r0 + C, seq - 8), 8)
    ext = jnp.concatenate([cur, x_ref[0, pl.ds(src, 8), :]], axis=0)
    last = jnp.where(r0 == seq - C, C - 1, -1)
    return jnp.where(rid == last, 0.0, pltpu.roll(ext, C + 7, 0)[0:C])


def _token_shift(x_ref, r0, seq, mu, ch0, ch_lo, ch_hi, grid_mode):
    cur = x_ref[0, pl.ds(r0, CHUNK), :]
    kinds = _shift_kinds(ch_lo, ch_hi, grid_mode)
    sh = _shifted(x_ref, r0, seq, kinds[-1][0])
    if len(kinds) > 1:
        cg = ch0 + lax.broadcasted_iota(jnp.int32, cur.shape, 1)
        for kind, hi in reversed(kinds[:-1]):
            sh = jnp.where(cg < hi, _shifted(x_ref, r0, seq, kind), sh)
    return cur + mu * (sh - cur)


_V_W0, _V_A0, _V_KK, _V_KA, _V_RK, _V_LNW, _V_LNB = 0, 2, 4, 5, 6, 7, 8


def _wkv_kernel(*refs, seq, unroll, grid_mode, has_s0, want_fin):
    it = iter(refs)
    r_ref, k_ref, v_ref, lo_ref, mu_ref, mul_ref, ww_ref, wa_ref, wgt_ref, vec_ref = (next(it) for _ in range(10))
    s0_ref = next(it) if has_s0 else None
    out_ref = next(it)
    sfin_ref = next(it) if want_fin else None
    q_s, lr_s, g_s, gam_s, ybuf, loa_s = it
    n_chunks = seq // CHUNK
    C = CHUNK
    pair = pl.program_id(1)
    vec = vec_ref[0]
    mu = mu_ref[0]

    @pl.when(pair == 0)
    def _():
        def lora_body(i, carry):
            r0 = pl.multiple_of(i * C, C)
            y = _token_shift(lo_ref, r0, seq, mul_ref[...], 3 * RWKV_WIDTH, 3 * RWKV_WIDTH, RWKV_IN_WIDTH, grid_mode)
            ln = lax.broadcasted_iota(jnp.int32, y.shape, 1)
            act = jnp.where(ln < DECAY_LORA, jnp.tanh(y),
                            jnp.where(ln < DECAY_LORA + AAA_LORA, y, jax.nn.sigmoid(y)))
            loa_s[pl.ds(r0, C), :] = act.astype(BF16)
            return carry

        lax.fori_loop(0, n_chunks, lora_body, 0)

    lane = lax.broadcasted_iota(jnp.int32, (C, LANES), 1)
    row = lax.broadcasted_iota(jnp.int32, (C, LANES), 0)
    head0 = lane < HEAD_SIZE
    eye = (lane % C == row).astype(F32)
    ri = lax.broadcasted_iota(jnp.int32, (LANES, LANES), 0)
    ci = lax.broadcasted_iota(jnp.int32, (LANES, LANES), 1)
    same_head = (ri // HEAD_SIZE) == (ci // HEAD_SIZE)
    seg2 = jnp.concatenate([same_head.astype(BF16)] * 2, axis=0)
    tr = lax.broadcasted_iota(jnp.int32, (C, 3 * C), 0)
    tc = lax.broadcasted_iota(jnp.int32, (C, 3 * C), 1) % C
    tri3 = ((tc <= tr).astype(BF16), (tc >= tr).astype(BF16))
    br = lax.broadcasted_iota(jnp.int32, (2 * C, 2 * LANES), 0)
    bc = lax.broadcasted_iota(jnp.int32, (2 * C, 2 * LANES), 1)
    brt, bct = br % C, bc % C
    incl = (br >= C).astype(jnp.int32)
    big_mask = (bct < brt + incl, bct > brt - incl)

    def stack(x):
        return jnp.concatenate([jnp.where(head0, x, 0.0), jnp.where(head0, 0.0, x)], axis=0).astype(BF16)

    def seg_sum(x):
        hi = x.astype(BF16)
        lo = (x - hi.astype(F32)).astype(BF16)
        return _dot(jnp.concatenate([hi, lo], axis=1), seg2)

    def cum_sum(d, x):
        return _dot(tri3[d], jnp.concatenate(_split3(x), axis=0))

    def each(fn, *lists):
        return [fn(*xs) for xs in zip(*lists)]

    def bf(xs):
        return [x.astype(BF16) for x in xs]

    def prep_group(chunks):
        rows = [pl.multiple_of(c * C, C) for c in chunks]
        ch0 = pair * LANES
        r = [_token_shift(r_ref, r0, seq, mu[0:1], ch0, 0, RWKV_WIDTH, grid_mode) for r0 in rows]
        k = [_token_shift(k_ref, r0, seq, mu[1:2], RWKV_WIDTH + ch0, RWKV_WIDTH, 2 * RWKV_WIDTH, grid_mode)
             for r0 in rows]
        v = [_token_shift(v_ref, r0, seq, mu[2:3], 2 * RWKV_WIDTH + ch0, 2 * RWKV_WIDTH, 3 * RWKV_WIDTH, grid_mode)
             for r0 in rows]
        lo = [loa_s[pl.ds(r0, C), 0:LANES] for r0 in rows]
        lw = each(lambda x: _dot(x, ww_ref[0]), lo)
        la = each(lambda x: _dot(x, wa_ref[0]), lo)
        kk = each(lambda x: x * vec[_V_KK:_V_KK + 1], k)
        ss = each(lambda x: seg_sum(x * x), kk)
        kk = each(lambda x, s: x * lax.rsqrt(jnp.maximum(s, 1e-24)), kk, ss)
        bonus = each(lambda rr, kx, vx: seg_sum(rr * kx * vec[_V_RK:_V_RK + 1]) * vx, r, k, v)
        vs = each(stack, v)
        ch = [(i, d) for i in range(len(chunks)) for d in range(2)]
        logw = [-jnp.exp(-_softplus(-(vec[_V_W0 + d:_V_W0 + d + 1] + lw[i][:, d * LANES:(d + 1) * LANES])) - 0.5)
                for i, d in ch]
        a = [jax.nn.sigmoid(vec[_V_A0 + d:_V_A0 + d + 1] + la[i][:, d * LANES:(d + 1) * LANES]) for i, d in ch]
        kd = [k[i] * (1.0 + (ax - 1.0) * vec[_V_KA:_V_KA + 1]) for (i, d), ax in zip(ch, a)]
        bt = [kk[i] * ax for (i, d), ax in zip(ch, a)]
        cl = [cum_sum(d, lx) for (i, d), lx in zip(ch, logw)]
        ltot = each(lambda lx: jnp.sum(lx, axis=0, keepdims=True), logw)
        e_neg = each(lambda x: jnp.exp(-x), cl)
        e_tot = each(lambda t, x: jnp.exp(t - x), ltot, cl)
        ahat = [-kk[i] * jnp.exp(cx - lx) for (i, d), cx, lx in zip(ch, cl, logw)]
        rhat = [r[i] * jnp.exp(cx) for (i, d), cx in zip(ch, cl)]
        bk = each(lambda b, kx, e: jnp.concatenate([stack(b * e), stack(kx * e)], axis=0), bt, kd, e_neg)
        bkc = each(lambda b, kx, e: jnp.concatenate([b * e, kx * e], axis=0).astype(BF16), bt, kd, e_tot)
        big = [jnp.where(big_mask[d], _dot_nt(jnp.concatenate([x, y], axis=0).astype(BF16), z), 0.0)
               for (i, d), x, y, z in zip(ch, ahat, rhat, bk)]
        n1 = [b[0:C, 0:LANES] for b in big]
        n1b, n1s = bf(n1), each(stack, n1)
        xyv = [_dot(b[:, LANES:].astype(BF16), vs[i]) for (i, d), b in zip(ch, big)]
        n2 = each(_dot, n1b, n1s)
        n2b, n2s = bf(n2), each(stack, n2)
        n4 = each(_dot, n2b, n2s)
        p1 = each(lambda x1, x2, x1b, x2s: eye + x1 + x2 + _dot(x1b, x2s), n1, n2, n1b, n2s)
        n4b, n4s = bf(n4), each(stack, n4)
        n8 = each(_dot, n4b, n4s)
        n8b, n8s = bf(n8), each(stack, n8)
        n16 = each(_dot, n8b, n8s)
        p2 = each(lambda x1, x2, x1b, x2s: eye + x1 + x2 + _dot(x1b, x2s), n4, n8, n4b, n8s)
        n16b, n16s = bf(n16), each(stack, n16)
        n32 = each(_dot, n16b, n16s)
        p12 = each(_dot, bf(p1), each(stack, p2))
        p3 = each(lambda x1, x2, x1b, x2s: eye + x1 + x2 + _dot(x1b, x2s), n16, n32, n16b, each(stack, n32))
        tb = bf(each(_dot, bf(p12), each(stack, p3)))
        au = each(lambda t, x, y: _dot(t, jnp.concatenate([stack(x), stack(y[0:C])], axis=1)),
                  tb, ahat, xyv)
        qy = each(lambda b, x: _dot(b[C:, 0:LANES].astype(BF16),
                                    jnp.concatenate([stack(x[:, 0:LANES]), stack(x[:, LANES:])], axis=1)),
                  big, au)
        aub = bf(au)
        lr = each(lambda x, y: _dot_tn(x[:, 0:LANES], y[0:C]), aub, bkc)
        g = [_dot_tn(jnp.concatenate([x[:, LANES:], v[i].astype(BF16)], axis=0), y)
             for (i, d), x, y in zip(ch, aub, bkc)]
        for (i, d), x, q, l, gx, t in zip(ch, rhat, qy, lr, g, ltot):
            c = chunks[i]
            q_s[d, c] = (x + q[:, 0:LANES]).astype(BF16)
            lr_s[d, c] = jnp.where(same_head, l, 0.0).astype(BF16)
            g_s[d, c] = jnp.where(same_head, gx, 0.0)
            gam_s[d, c] = jnp.broadcast_to(jnp.exp(t), (8, LANES))
        for i, r0 in enumerate(rows):
            y = bonus[i]
            for d in range(2):
                y = y + xyv[2 * i + d][C:] + qy[2 * i + d][:, LANES:]
            ybuf[pl.ds(r0, C), :] = y

    def prep_body(i, carry):
        prep_group([i * unroll + u for u in range(unroll)])
        return carry

    lax.fori_loop(0, n_chunks // unroll, prep_body, 0)

    def step_body(i, carry):
        cs = (i, n_chunks - 1 - i)
        sb = bf(carry)
        ys = [_dot_nt(q_s[d, cs[d]], sb[d]) for d in range(2)]
        ls = [_dot(sb[d], lr_s[d, cs[d]]) for d in range(2)]
        for d in range(2):
            r0 = pl.multiple_of(cs[d] * C, C)
            ybuf[pl.ds(r0, C), :] += ys[d]
        return tuple(carry[d] * gam_s[d, cs[d]][0:1] + ls[d] + g_s[d, cs[d]] for d in range(2))

    zero_h = jnp.zeros((HEAD_SIZE, HEAD_SIZE), F32)

    def pair_state(s2):
        return jnp.concatenate([jnp.concatenate([s2[0], zero_h], axis=1),
                                jnp.concatenate([zero_h, s2[1]], axis=1)], axis=0)

    if has_s0:
        init = (pair_state(s0_ref[0, 0]), pair_state(s0_ref[0, 1]))
    else:
        init = (jnp.zeros((LANES, LANES), F32),) * 2
    fin = lax.fori_loop(0, n_chunks, step_body, init)
    if want_fin:
        for d in range(2):
            sfin_ref[0, d, 0] = fin[d][0:HEAD_SIZE, 0:HEAD_SIZE]
            sfin_ref[0, d, 1] = fin[d][HEAD_SIZE:, HEAD_SIZE:]

    def out_body(i, carry):
        rows = [pl.multiple_of((i * unroll + u) * C, C) for u in range(unroll)]
        y = [ybuf[pl.ds(r0, C), :] for r0 in rows]
        g = [_dot(loa_s[pl.ds(r0, C), LANES:LORA_OUT], wgt_ref[0]) for r0 in rows]
        mean = each(lambda x: seg_sum(x) * (1.0 / HEAD_SIZE), y)
        dlt = each(lambda x, m: x - m, y, mean)
        var = each(lambda x: seg_sum(x * x) * (1.0 / HEAD_SIZE), dlt)
        for r0, dx, vx, gx in zip(rows, dlt, var, g):
            yn = dx * lax.rsqrt(vx + GN_EPS) * vec[_V_LNW:_V_LNW + 1] + vec[_V_LNB:_V_LNB + 1]
            out_ref[0, pl.ds(r0, C), :] = (yn * gx).astype(BF16)
        return carry

    lax.fori_loop(0, n_chunks // unroll, out_body, 0)


def _wkv(proj3, mu3, mu_lora, ww, wa, wgt, vec, s0, grid_mode, want_fin):
    nb, seq, _ = proj3.shape
    n_chunks = seq // CHUNK
    has_s0 = s0 is not None
    kern = functools.partial(_wkv_kernel, seq=seq, unroll=min(8, n_chunks), grid_mode=grid_mode, has_s0=has_s0,
                             want_fin=want_fin)
    state_spec = pl.BlockSpec((1, 2, 2, HEAD_SIZE, HEAD_SIZE), lambda b, j: (b, 0, j, 0, 0))
    in_specs = [pl.BlockSpec((1, seq, LANES), lambda b, j: (b, 0, R_BLK + j)),
                pl.BlockSpec((1, seq, LANES), lambda b, j: (b, 0, K_BLK + j)),
                pl.BlockSpec((1, seq, LANES), lambda b, j: (b, 0, V_BLK + j)),
                pl.BlockSpec((1, seq, LORA_OUT), lambda b, j: (b, 0, 0)),
                pl.BlockSpec((1, 3, LANES), lambda b, j: (j, 0, 0)),
                pl.BlockSpec((1, LORA_OUT), lambda b, j: (0, 0)),
                pl.BlockSpec((1, LANES, 2 * LANES), lambda b, j: (j, 0, 0)),
                pl.BlockSpec((1, LANES, 2 * LANES), lambda b, j: (j, 0, 0)),
                pl.BlockSpec((1, 2 * LANES, LANES), lambda b, j: (j, 0, 0)),
                pl.BlockSpec((1, 16, LANES), lambda b, j: (j, 0, 0))]
    args = [proj3, proj3, proj3, proj3, mu3, mu_lora, ww, wa, wgt, vec]
    out_specs = [pl.BlockSpec((1, seq, LANES), lambda b, j: (b, 0, j))]
    out_shape = [jax.ShapeDtypeStruct((nb, seq, RWKV_WIDTH), BF16)]
    if has_s0:
        in_specs.append(state_spec)
        args.append(s0)
    if want_fin:
        out_specs.append(state_spec)
        out_shape.append(jax.ShapeDtypeStruct((nb, 2, RWKV_HEADS, HEAD_SIZE, HEAD_SIZE), F32))
    res = pl.pallas_call(
        kern,
        grid=(nb, PAIRS),
        in_specs=in_specs,
        out_specs=out_specs,
        out_shape=out_shape,
        scratch_shapes=[pltpu.VMEM((2, n_chunks, CHUNK, LANES), BF16),
                        pltpu.VMEM((2, n_chunks, LANES, LANES), BF16),
                        pltpu.VMEM((2, n_chunks, LANES, LANES), F32),
                        pltpu.VMEM((2, n_chunks, 8, LANES), F32),
                        pltpu.VMEM((seq, LANES), F32),
                        pltpu.VMEM((seq, LORA_OUT), BF16)],
        compiler_params=_cparams(("arbitrary", "arbitrary")),
        name="wkv",
    )(*args)
    return res[0], (res[1] if want_fin else None)


def _out_proj_kernel(lru_ref, rw_ref, w1_ref, w2_ref, x_ref, mod_ref, g_ref, x1_ref, h2_ref):
    out = _dot(lru_ref[...], w1_ref[...]) + _dot(rw_ref[...], w2_ref[...])
    m = mod_ref[0]
    g = g_ref[...]
    x1 = x_ref[...] + m[2:3] * _rms(out, g[0:1])
    x1_ref[...] = x1
    h2_ref[...] = (_rms(x1, g[1:2]) * (1.0 + m[4:5]) + m[3:4]).astype(BF16)


def _out_proj(lru_o, rw_o, w_out_b, x2d, mod3, g2, mod_idx, tm=512):
    m = x2d.shape[0]
    half = D_MODEL // 2
    return pl.pallas_call(
        _out_proj_kernel,
        grid=(m // tm,),
        in_specs=[pl.BlockSpec((tm, half), lambda i: (i, 0)),
                  pl.BlockSpec((tm, half), lambda i: (i, 0)),
                  pl.BlockSpec((half, D_MODEL), lambda i: (0, 0)),
                  pl.BlockSpec((half, D_MODEL), lambda i: (1, 0)),
                  pl.BlockSpec((tm, D_MODEL), lambda i: (i, 0)),
                  pl.BlockSpec((1, 6, D_MODEL), lambda i: (mod_idx(i * tm), 0, 0)),
                  pl.BlockSpec((2, D_MODEL), lambda i: (0, 0))],
        out_specs=[pl.BlockSpec((tm, D_MODEL), lambda i: (i, 0)),
                   pl.BlockSpec((tm, D_MODEL), lambda i: (i, 0))],
        out_shape=[jax.ShapeDtypeStruct((m, D_MODEL), F32),
                   jax.ShapeDtypeStruct((m, D_MODEL), BF16)],
        compiler_params=_cparams(("parallel",)),
        name="out_proj",
    )(lru_o, rw_o, w_out_b, w_out_b, x2d, mod3, g2)


def _ffn_kernel(h_ref, wg_ref, wu_ref, wd_ref, x1_ref, mod_ref, g_ref, o_ref, acc):
    f = pl.program_id(1)
    h = h_ref[...]
    gate = _dot(h, wg_ref[...])
    up = _dot(h, wu_ref[...])
    act = (gate * jax.nn.sigmoid(gate) * up).astype(BF16)
    part = _dot(act, wd_ref[...])

    @pl.when(f == 0)
    def _():
        acc[...] = part

    @pl.when(f > 0)
    def _():
        acc[...] += part

    @pl.when(f == pl.num_programs(1) - 1)
    def _():
        o_ref[...] = x1_ref[...] + mod_ref[0][5:6] * _rms(acc[...], g_ref[...])


def _ffn(h2, w_gu_b, w_down_b, x1, mod3, g, mod_idx, tm=512, tf=512):
    m = h2.shape[0]
    nf = FFN_HIDDEN // tf
    return pl.pallas_call(
        _ffn_kernel,
        grid=(m // tm, nf),
        in_specs=[pl.BlockSpec((tm, D_MODEL), lambda i, f: (i, 0)),
                  pl.BlockSpec((D_MODEL, tf), lambda i, f: (0, f)),
                  pl.BlockSpec((D_MODEL, tf), lambda i, f: (0, nf + f)),
                  pl.BlockSpec((tf, D_MODEL), lambda i, f: (f, 0)),
                  pl.BlockSpec((tm, D_MODEL), lambda i, f: (i, 0)),
                  pl.BlockSpec((1, 6, D_MODEL), lambda i, f: (mod_idx(i * tm), 0, 0)),
                  pl.BlockSpec((1, D_MODEL), lambda i, f: (0, 0))],
        out_specs=pl.BlockSpec((tm, D_MODEL), lambda i, f: (i, 0)),
        out_shape=jax.ShapeDtypeStruct((m, D_MODEL), F32),
        scratch_shapes=[pltpu.VMEM((tm, D_MODEL), F32)],
        compiler_params=_cparams(("parallel", "arbitrary")),
        name="ffn",
    )(h2, w_gu_b, w_gu_b, w_down_b, x1, mod3, g)


def _group(x3, mod3, mod_idx, grid_mode, lru_h0, wkv_s0, want_fin, P):
    nb, seq, _ = x3.shape
    x2d = x3.reshape(nb * seq, D_MODEL)
    proj = _in_proj(x2d, mod3, P["norm_mix_pre"], P["w_in"], mod_idx).reshape(nb, seq, PROJ_PAD)
    lru_o, lru_fin = _lru(proj, P["conv_w"], P["conv_b"], P["lru_wg"], P["lru_bg"], P["lru_lam"], lru_h0)
    rw_o, s_fin = _wkv(proj, P["mu3"], P["mu_lora"], P["ww"], P["wa"], P["wgt"], P["vec"], wkv_s0,
                       grid_mode, want_fin)
    x1, h2 = _out_proj(lru_o.reshape(nb * seq, LRU_WIDTH), rw_o.reshape(nb * seq, RWKV_WIDTH),
                       P["w_out"], x2d, mod3, P["g_post_pre"], mod_idx)
    y = _ffn(h2, P["w_gu"], P["w_down"], x1, mod3, P["norm_ffn_post"], mod_idx)
    return y.reshape(nb, seq, D_MODEL), lru_fin, s_fin


def _prep_params(norm_mix_pre, norm_mix_post, norm_ffn_pre, norm_ffn_post, w_in,
                 lru_conv_w, lru_conv_b, lru_wr, lru_br, lru_wi, lru_bi, lru_lambda,
                 rwkv_mu, rwkv_w0, rwkv_w_up, rwkv_a0, rwkv_a_up, rwkv_g_up, rwkv_k_k, rwkv_k_a, rwkv_r_k,
                 rwkv_ln_w, rwkv_ln_b, w_out, ffn_w_gu, ffn_w_down):
    P = {}
    P["norm_mix_pre"] = norm_mix_pre.reshape(1, D_MODEL)
    P["g_post_pre"] = jnp.stack([norm_mix_post, norm_ffn_pre], axis=0)
    P["norm_ffn_post"] = norm_ffn_post.reshape(1, D_MODEL)
    w_in_b = w_in.astype(BF16)
    n_main = IN_WIDTH - LORA_WIDTH
    P["w_in"] = jnp.concatenate([w_in_b[:, n_main:], jnp.zeros((D_MODEL, LORA_PAD - LORA_WIDTH), BF16),
                                 w_in_b[:, :n_main]], axis=1)
    P["conv_w"] = lru_conv_w
    P["conv_b"] = lru_conv_b.reshape(1, LRU_WIDTH)
    P["lru_wg"] = jnp.concatenate([lru_wr[0], lru_wi[0], lru_wr[1], lru_wi[1]], axis=-1).astype(BF16)
    P["lru_bg"] = jnp.concatenate([lru_br[0], lru_bi[0], lru_br[1], lru_bi[1]], axis=-1).reshape(LRU_HEADS, 1, 512)
    P["lru_lam"] = jnp.swapaxes(lru_lambda.reshape(2, LRU_HEADS, LRU_HEAD_DIM), 0, 1)
    P["mu3"] = jnp.swapaxes(rwkv_mu[:3 * RWKV_WIDTH].reshape(3, PAIRS, LANES), 0, 1)
    P["mu_lora"] = jnp.pad(rwkv_mu[3 * RWKV_WIDTH:], (0, LORA_OUT - LORA_WIDTH)).reshape(1, LORA_OUT)

    def per_pair(w):
        return jnp.swapaxes(w.reshape(w.shape[0], PAIRS, LANES), 0, 1)

    wu = jnp.concatenate([per_pair(rwkv_w_up[0]), per_pair(rwkv_w_up[1])], axis=-1)
    au = jnp.concatenate([per_pair(rwkv_a_up[0]), per_pair(rwkv_a_up[1])], axis=-1)
    P["ww"] = jnp.pad(wu, ((0, 0), (0, LANES - DECAY_LORA), (0, 0))).astype(BF16)
    P["wa"] = jnp.pad(au, ((0, 0), (DECAY_LORA, LANES - DECAY_LORA - AAA_LORA), (0, 0))).astype(BF16)
    P["wgt"] = jnp.pad(per_pair(rwkv_g_up), ((0, 0), (0, 2 * LANES - GATE_LORA), (0, 0))).astype(BF16)
    rows = [rwkv_w0[0], rwkv_w0[1], rwkv_a0[0], rwkv_a0[1], rwkv_k_k, rwkv_k_a,
            rwkv_r_k.reshape(RWKV_WIDTH), rwkv_ln_w, rwkv_ln_b]
    vec = jnp.stack(rows + [jnp.zeros_like(rwkv_k_k)] * (16 - len(rows)), axis=0)
    P["vec"] = per_pair(vec)
    P["w_out"] = w_out.astype(BF16)
    P["w_gu"] = ffn_w_gu.astype(BF16)
    P["w_down"] = ffn_w_down.astype(BF16)
    return P


def kernel(x_prompt, x_sample, state_lru, state_wkv, c, c_ctx, norm_mix_pre, norm_mix_post, norm_ffn_pre,
           norm_ffn_post, w_mod, b_mod, w_in, lru_conv_w, lru_conv_b, lru_wr, lru_br, lru_wi, lru_bi,
           lru_lambda, rwkv_mu, rwkv_w0, rwkv_w_up, rwkv_a0, rwkv_a_up, rwkv_g_up, rwkv_k_k, rwkv_k_a,
           rwkv_r_k, rwkv_ln_w, rwkv_ln_b, w_out, ffn_w_gu, ffn_w_down):
    depth = w_in.shape[0]
    nb_p = x_prompt.shape[0]
    nb_s, seq_s, _ = x_sample.shape
    y_p, y_s = x_prompt, x_sample
    new_lru, new_wkv = [], []
    cvec = jnp.concatenate([c_ctx[None, :], c, jnp.zeros((8 - 1 - nb_s, D_MODEL), F32)], axis=0)
    for l in range(depth):
        P = _prep_params(norm_mix_pre[l], norm_mix_post[l], norm_ffn_pre[l], norm_ffn_post[l], w_in[l],
                         lru_conv_w[l], lru_conv_b[l], lru_wr[l], lru_br[l], lru_wi[l], lru_bi[l],
                         lru_lambda[l], rwkv_mu[l], rwkv_w0[l], rwkv_w_up[l], rwkv_a0[l], rwkv_a_up[l],
                         rwkv_g_up[l], rwkv_k_k[l], rwkv_k_a[l], rwkv_r_k[l], rwkv_ln_w[l], rwkv_ln_b[l],
                         w_out[l], ffn_w_gu[l], ffn_w_down[l])
        mod3 = _mod(cvec, w_mod[l], b_mod[l].reshape(1, -1)).reshape(8, 6, D_MODEL)
        y_p, lru_ctx, wkv_ctx = _group(
            y_p, mod3, lambda row: 0, False, jnp.zeros((nb_p, 2, LRU_WIDTH), F32), None, True, P)
        new_lru.append(lru_ctx)
        new_wkv.append(wkv_ctx)
        y_s, _, _ = _group(
            y_s, mod3, lambda row: 1 + row // seq_s, True, state_lru[:, l], state_wkv[:, l], False, P)
    return (y_p, y_s, jnp.stack(new_lru, axis=1), jnp.stack(new_wkv, axis=1))
```

```python
import functools

import jax
import jax.numpy as jnp
from jax import lax
from jax.experimental import pallas as pl
from jax.experimental.pallas import tpu as pltpu

F32 = jnp.float32
BF16 = jnp.bfloat16

D_MODEL = 2048
LRU_WIDTH = 1024
LRU_HEADS = 8
LRU_HEAD_DIM = 128
LRU_C = 8.0
RWKV_WIDTH = 1024
HEAD_SIZE = 64
RWKV_HEADS = 16
DECAY_LORA = 64
AAA_LORA = 64
GATE_LORA = 160
LORA_WIDTH = DECAY_LORA + AAA_LORA + GATE_LORA
RWKV_IN_WIDTH = 3 * RWKV_WIDTH + LORA_WIDTH
IN_WIDTH = 2 * LRU_WIDTH + RWKV_IN_WIDTH
FFN_HIDDEN = 5632
GRID_W = 64
RMS_EPS = 1e-6
GN_EPS = 64e-5

LANES = 128
LORA_OUT = 384
LORA_PAD = 512
PROJ_PAD = LORA_PAD + 2 * LRU_WIDTH + 3 * RWKV_WIDTH
XL_BLK = LORA_PAD // 128
GL_BLK = XL_BLK + LRU_WIDTH // 128
R_BLK = GL_BLK + LRU_WIDTH // 128
K_BLK = R_BLK + RWKV_WIDTH // 128
V_BLK = K_BLK + RWKV_WIDTH // 128
CHUNK = 64
PAIRS = RWKV_HEADS // 2
VMEM_LIMIT = 56 * 1024 * 1024


def _cparams(sem, flags=None):
    return pltpu.CompilerParams(dimension_semantics=sem, vmem_limit_bytes=VMEM_LIMIT, flags=flags)


def _dot(a, b):
    return jnp.dot(a, b, preferred_element_type=F32)


def _dot_nt(a, b):
    return lax.dot_general(a, b, (((1,), (1,)), ((), ())), preferred_element_type=F32)


def _dot_tn(a, b):
    return lax.dot_general(a, b, (((0,), (0,)), ((), ())), preferred_element_type=F32)


def _split3(x):
    hi = x.astype(BF16)
    r1 = x - hi.astype(F32)
    mid = r1.astype(BF16)
    lo = (r1 - mid.astype(F32)).astype(BF16)
    return hi, mid, lo


def _softplus(x):
    return jnp.maximum(x, 0.0) + jnp.log1p(jnp.exp(-jnp.abs(x)))


def _rms(x, g):
    return x * lax.rsqrt(jnp.mean(x * x, axis=-1, keepdims=True) + RMS_EPS) * g


def _mod_kernel(c_ref, w_ref, b_ref, o_ref):
    c = c_ref[...]
    s = c * jax.nn.sigmoid(c)
    o_ref[...] = jnp.dot(s, w_ref[...], precision=lax.Precision.HIGHEST,
                         preferred_element_type=F32) + b_ref[...]


def _mod(cvec, w_mod, b_mod):
    n = w_mod.shape[1]
    tn = 1024
    return pl.pallas_call(
        _mod_kernel,
        grid=(n // tn,),
        in_specs=[pl.BlockSpec((8, D_MODEL), lambda j: (0, 0)),
                  pl.BlockSpec((D_MODEL, tn), lambda j: (0, j)),
                  pl.BlockSpec((1, tn), lambda j: (0, j))],
        out_specs=pl.BlockSpec((8, tn), lambda j: (0, j)),
        out_shape=jax.ShapeDtypeStruct((8, n), F32),
        compiler_params=_cparams(("parallel",)),
        name="mod",
    )(cvec, w_mod, b_mod)


def _in_proj_kernel(x_ref, mod_ref, g_ref, w_ref, o_ref, h_scr):
    @pl.when(pl.program_id(1) == 0)
    def _():
        m = mod_ref[0]
        h = _rms(x_ref[...], g_ref[...]) * (1.0 + m[1:2]) + m[0:1]
        h_scr[...] = h.astype(BF16)

    o_ref[...] = _dot(h_scr[...], w_ref[...])


def _in_proj(x2d, mod3, g, w_in_p, mod_idx, tm=1024, tn=1408):
    m = x2d.shape[0]
    return pl.pallas_call(
        _in_proj_kernel,
        grid=(m // tm, PROJ_PAD // tn),
        in_specs=[pl.BlockSpec((tm, D_MODEL), lambda i, j: (i, 0)),
                  pl.BlockSpec((1, 6, D_MODEL), lambda i, j: (mod_idx(i * tm), 0, 0)),
                  pl.BlockSpec((1, D_MODEL), lambda i, j: (0, 0)),
                  pl.BlockSpec((D_MODEL, tn), lambda i, j: (0, j))],
        out_specs=pl.BlockSpec((tm, tn), lambda i, j: (i, j)),
        out_shape=jax.ShapeDtypeStruct((m, PROJ_PAD), F32),
        scratch_shapes=[pltpu.VMEM((tm, D_MODEL), BF16)],
        compiler_params=_cparams(("parallel", "arbitrary")),
        name="in_proj",
    )(x2d, mod3, g, w_in_p)


def _lru_kernel(xl_ref, gl_ref, cw_ref, cb_ref, wg_ref, bg_ref, lam_ref, h0_ref,
                out_ref, hfin_ref, xpad, a_f, b_f, a_b, b_b, *, seq, tile):
    n_tiles = seq // tile
    zeros8 = jnp.zeros((8, LANES), F32)
    xpad[pl.ds(0, 8), :] = zeros8
    xpad[pl.ds(seq + 8, 8), :] = zeros8

    def copy_body(i, c):
        r0 = pl.multiple_of(i * tile, tile)
        xpad[pl.ds(r0 + 8, tile), :] = xl_ref[0, pl.ds(r0, tile), :]
        return c

    lax.fori_loop(0, n_tiles, copy_body, 0)

    cneg = -LRU_C * _softplus(-lam_ref[0])
    cw = cw_ref[...]
    cb = cb_ref[...]
    bg = bg_ref[0]

    def gate_body(i, c):
        r0 = pl.multiple_of(i * tile, tile)
        ext = xpad[pl.ds(r0, tile + 16), :]
        n_ext = tile + 16
        xc = (pltpu.roll(ext, 2, 0)[8:8 + tile] * cw[0:1]
              + pltpu.roll(ext, 1, 0)[8:8 + tile] * cw[1:2]
              + ext[8:8 + tile] * cw[2:3]
              + pltpu.roll(ext, n_ext - 1, 0)[8:8 + tile] * cw[3:4]) + cb
        g = _dot(xc.astype(BF16), wg_ref[0]) + bg
        for d, (a_s, b_s) in enumerate(((a_f, b_f), (a_b, b_b))):
            r = jax.nn.sigmoid(g[:, 256 * d:256 * d + 128])
            ig = jax.nn.sigmoid(g[:, 256 * d + 128:256 * d + 256])
            log_a = cneg[d:d + 1] * r
            a = jnp.exp(log_a)
            one_m_a2 = -jnp.tanh(log_a) * (a * a + 1.0)
            a_s[pl.ds(r0, tile), :] = a
            b_s[pl.ds(r0, tile), :] = jnp.sqrt(one_m_a2) * (ig * xc)
        return c

    lax.fori_loop(0, n_tiles, gate_body, 0)

    group = 4
    n_steps = seq // (8 * group)
    rid = lax.broadcasted_iota(jnp.int32, (8, LANES), 0)

    def tile_scan(a, b, reverse):
        for s in (1, 2, 4):
            if reverse:
                keep = rid < 8 - s
                a_sh = jnp.where(keep, pltpu.roll(a, 8 - s, 0), 1.0)
                b_sh = jnp.where(keep, pltpu.roll(b, 8 - s, 0), 0.0)
            else:
                keep = rid >= s
                a_sh = jnp.where(keep, pltpu.roll(a, s, 0), 1.0)
                b_sh = jnp.where(keep, pltpu.roll(b, s, 0), 0.0)
            b = a * b_sh + b
            a = a * a_sh
        return a, b

    def scan_body(i, carry):
        hf, hb = carry
        rf = [pl.multiple_of((i * group + u) * 8, 8) for u in range(group)]
        rb = [pl.multiple_of(((n_steps - 1 - i) * group + (group - 1 - u)) * 8, 8) for u in range(group)]
        sf = [tile_scan(a_f[pl.ds(r, 8), :], b_f[pl.ds(r, 8), :], False) for r in rf]
        sb = [tile_scan(a_b[pl.ds(r, 8), :], b_b[pl.ds(r, 8), :], True) for r in rb]
        for u in range(group):
            h = sf[u][1] + sf[u][0] * hf
            b_f[pl.ds(rf[u], 8), :] = h
            hf = h[7:8]
            h = sb[u][1] + sb[u][0] * hb
            b_b[pl.ds(rb[u], 8), :] = h
            hb = h[0:1]
        return hf, hb

    h0 = h0_ref[0]
    hf, hb = lax.fori_loop(0, n_steps, scan_body, (h0[0:1], h0[1:2]))
    hfin_ref[0] = jnp.concatenate([hf, hb], axis=0)

    def out_body(i, c):
        r0 = pl.multiple_of(i * tile, tile)
        gl = gl_ref[0, pl.ds(r0, tile), :]
        hs = b_f[pl.ds(r0, tile), :] + b_b[pl.ds(r0, tile), :]
        out_ref[0, pl.ds(r0, tile), :] = (hs * jax.nn.gelu(gl)).astype(BF16)
        return c

    lax.fori_loop(0, n_tiles, out_body, 0)


def _lru(proj3, conv_w, conv_b, wg, bg, lam, h0):
    nb, seq, _ = proj3.shape
    tile = 256
    kern = functools.partial(_lru_kernel, seq=seq, tile=tile)
    return pl.pallas_call(
        kern,
        grid=(nb, LRU_HEADS),
        in_specs=[pl.BlockSpec((1, seq, LANES), lambda b, h: (b, 0, XL_BLK + h)),
                  pl.BlockSpec((1, seq, LANES), lambda b, h: (b, 0, GL_BLK + h)),
                  pl.BlockSpec((4, LANES), lambda b, h: (0, h)),
                  pl.BlockSpec((1, LANES), lambda b, h: (0, h)),
                  pl.BlockSpec((1, LANES, 512), lambda b, h: (h, 0, 0)),
                  pl.BlockSpec((1, 1, 512), lambda b, h: (h, 0, 0)),
                  pl.BlockSpec((1, 2, LANES), lambda b, h: (h, 0, 0)),
                  pl.BlockSpec((1, 2, LANES), lambda b, h: (b, 0, h))],
        out_specs=[pl.BlockSpec((1, seq, LANES), lambda b, h: (b, 0, h)),
                   pl.BlockSpec((1, 2, LANES), lambda b, h: (b, 0, h))],
        out_shape=[jax.ShapeDtypeStruct((nb, seq, LRU_WIDTH), BF16),
                   jax.ShapeDtypeStruct((nb, 2, LRU_WIDTH), F32)],
        scratch_shapes=[pltpu.VMEM((seq + 16, LANES), F32)] + [pltpu.VMEM((seq, LANES), F32)] * 4,
        compiler_params=_cparams(("parallel", "arbitrary")),
        name="lru",
    )(proj3, proj3, conv_w, conv_b, wg, bg, lam, h0)


def _shift_kinds(ch_lo, ch_hi, grid_mode):
    if grid_mode:
        q = RWKV_IN_WIDTH // 4
        parts = [("left", q), ("right", 2 * q), ("up", 3 * q), ("down", RWKV_IN_WIDTH)]
    else:
        parts = [("prev", RWKV_IN_WIDTH // 2), ("next", RWKV_IN_WIDTH)]
    kinds, lo = [], 0
    for kind, hi in parts:
        if ch_lo < hi and ch_hi > lo:
            kinds.append((kind, hi))
        lo = hi
    return kinds


def _shifted(x_ref, r0, seq, kind):
    C = CHUNK
    cur = x_ref[pl.ds(r0, C), :]
    rid = lax.broadcasted_iota(jnp.int32, cur.shape, 0)
    if kind == "left":
        return jnp.where(rid == 0, 0.0, pltpu.roll(cur, 1, 0))
    if kind == "right":
        return jnp.where(rid == C - 1, 0.0, pltpu.roll(cur, C - 1, 0))
    if kind == "up":
        src = pl.multiple_of(jnp.maximum(r0 - C, 0), C)
        return jnp.where(r0 > 0, x_ref[pl.ds(src, C), :], 0.0)
    if kind == "down":
        src = pl.multiple_of(jnp.minimum(r0 + C, seq - C), C)
        return jnp.where(r0 < seq - C, x_ref[pl.ds(src, C), :], 0.0)
    if kind == "prev":
        src = pl.multiple_of(jnp.maximum(r0 - 8, 0), 8)
        ext = jnp.concatenate([x_ref[pl.ds(src, 8), :], cur], axis=0)
        first = jnp.where(r0 == 0, 0, -1)
        return jnp.where(rid == first, 0.0, pltpu.roll(ext, 1, 0)[8:8 + C])
    assert kind == "next"
    src = pl.multiple_of(jnp.minimum(r0 + C, seq - 8), 8)
    ext = jnp.concatenate([cur, x_ref[pl.ds(src, 8), :]], axis=0)
    last = jnp.where(r0 == seq - C, C - 1, -1)
    return jnp.where(rid == last, 0.0, pltpu.roll(ext, C + 7, 0)[0:C])


def _token_shift(x_ref, r0, seq, mu, ch0, ch_lo, ch_hi, grid_mode):
    cur = x_ref[pl.ds(r0, CHUNK), :]
    kinds = _shift_kinds(ch_lo, ch_hi, grid_mode)
    sh = _shifted(x_ref, r0, seq, kinds[-1][0])
    if len(kinds) > 1:
        cg = ch0 + lax.broadcasted_iota(jnp.int32, cur.shape, 1)
        for kind, hi in reversed(kinds[:-1]):
            sh = jnp.where(cg < hi, _shifted(x_ref, r0, seq, kind), sh)
    return cur + mu * (sh - cur)


_V_W0, _V_A0, _V_KK, _V_KA, _V_RK, _V_LNW, _V_LNB = 0, 2, 4, 5, 6, 7, 8


def _wkv_kernel(*refs, seq, nseq, unroll, grid_mode, has_s0, want_fin):
    it = iter(refs)
    r_ref, k_ref, v_ref, lo_ref, mu_ref, mul_ref, ww_ref, wa_ref, wgt_ref, vec_ref = (next(it) for _ in range(10))
    s0_ref = next(it) if has_s0 else None
    out_ref = next(it)
    sfin_ref = next(it) if want_fin else None
    q_s, lr_s, g_s, gam_s, ybuf, loa_s = it
    n_chunks = seq // CHUNK
    C = CHUNK
    pair = pl.program_id(1)
    vec = vec_ref[0]
    mu = mu_ref[0]

    @pl.when(pair == 0)
    def _():
        def lora_body(i, carry):
            r0 = pl.multiple_of(i * C, C)
            for s in range(nseq):
                y = _token_shift(lo_ref.at[s], r0, seq, mul_ref[...], 3 * RWKV_WIDTH, 3 * RWKV_WIDTH,
                                 RWKV_IN_WIDTH, grid_mode)
                ln = lax.broadcasted_iota(jnp.int32, y.shape, 1)
                act = jnp.where(ln < DECAY_LORA, jnp.tanh(y),
                                jnp.where(ln < DECAY_LORA + AAA_LORA, y, jax.nn.sigmoid(y)))
                loa_s[pl.ds(s * seq + r0, C), :] = act.astype(BF16)
            return carry

        lax.fori_loop(0, n_chunks, lora_body, 0)

    lane = lax.broadcasted_iota(jnp.int32, (C, LANES), 1)
    row = lax.broadcasted_iota(jnp.int32, (C, LANES), 0)
    head0 = lane < HEAD_SIZE
    eye = (lane % C == row).astype(F32)
    ri = lax.broadcasted_iota(jnp.int32, (LANES, LANES), 0)
    ci = lax.broadcasted_iota(jnp.int32, (LANES, LANES), 1)
    same_head = (ri // HEAD_SIZE) == (ci // HEAD_SIZE)
    seg2 = jnp.concatenate([same_head.astype(BF16)] * 2, axis=0)
    tr = lax.broadcasted_iota(jnp.int32, (C, 3 * C), 0)
    tc = lax.broadcasted_iota(jnp.int32, (C, 3 * C), 1) % C
    tri3 = ((tc <= tr).astype(BF16), (tc >= tr).astype(BF16))
    br = lax.broadcasted_iota(jnp.int32, (2 * C, 2 * LANES), 0)
    bc = lax.broadcasted_iota(jnp.int32, (2 * C, 2 * LANES), 1)
    brt, bct = br % C, bc % C
    incl = (br >= C).astype(jnp.int32)
    big_mask = (bct < brt + incl, bct > brt - incl)

    def stack(x):
        return jnp.concatenate([jnp.where(head0, x, 0.0), jnp.where(head0, 0.0, x)], axis=0).astype(BF16)

    def seg_sum(x):
        hi = x.astype(BF16)
        lo = (x - hi.astype(F32)).astype(BF16)
        return _dot(jnp.concatenate([hi, lo], axis=1), seg2)

    def cum_sum(d, x):
        return _dot(tri3[d], jnp.concatenate(_split3(x), axis=0))

    def each(fn, *lists):
        return [fn(*xs) for xs in zip(*lists)]

    def bf(xs):
        return [x.astype(BF16) for x in xs]

    def prep_group(chunks):
        seqs = [s for s, c in chunks]
        rows = [pl.multiple_of(c * C, C) for s, c in chunks]
        flat = [pl.multiple_of(s * seq + c * C, C) for s, c in chunks]
        cid = [s * n_chunks + c for s, c in chunks]
        ch0 = pair * LANES
        r = [_token_shift(r_ref.at[s], r0, seq, mu[0:1], ch0, 0, RWKV_WIDTH, grid_mode) for s, r0 in zip(seqs, rows)]
        k = [_token_shift(k_ref.at[s], r0, seq, mu[1:2], RWKV_WIDTH + ch0, RWKV_WIDTH, 2 * RWKV_WIDTH, grid_mode)
             for s, r0 in zip(seqs, rows)]
        v = [_token_shift(v_ref.at[s], r0, seq, mu[2:3], 2 * RWKV_WIDTH + ch0, 2 * RWKV_WIDTH, 3 * RWKV_WIDTH,
                          grid_mode) for s, r0 in zip(seqs, rows)]
        lo = [loa_s[pl.ds(f0, C), 0:LANES] for f0 in flat]
        lw = each(lambda x: _dot(x, ww_ref[0]), lo)
        la = each(lambda x: _dot(x, wa_ref[0]), lo)
        kk = each(lambda x: x * vec[_V_KK:_V_KK + 1], k)
        ss = each(lambda x: seg_sum(x * x), kk)
        kk = each(lambda x, s: x * lax.rsqrt(jnp.maximum(s, 1e-24)), kk, ss)
        bonus = each(lambda rr, kx, vx: seg_sum(rr * kx * vec[_V_RK:_V_RK + 1]) * vx, r, k, v)
        vs = each(stack, v)
        ch = [(i, d) for i in range(len(chunks)) for d in range(2)]
        logw = [-jnp.exp(-_softplus(-(vec[_V_W0 + d:_V_W0 + d + 1] + lw[i][:, d * LANES:(d + 1) * LANES])) - 0.5)
                for i, d in ch]
        a = [jax.nn.sigmoid(vec[_V_A0 + d:_V_A0 + d + 1] + la[i][:, d * LANES:(d + 1) * LANES]) for i, d in ch]
        kd = [k[i] * (1.0 + (ax - 1.0) * vec[_V_KA:_V_KA + 1]) for (i, d), ax in zip(ch, a)]
        bt = [kk[i] * ax for (i, d), ax in zip(ch, a)]
        cl = [cum_sum(d, lx) for (i, d), lx in zip(ch, logw)]
        ltot = each(lambda lx: jnp.sum(lx, axis=0, keepdims=True), logw)
        e_neg = each(lambda x: jnp.exp(-x), cl)
        e_tot = each(lambda t, x: jnp.exp(t - x), ltot, cl)
        ahat = [-kk[i] * jnp.exp(cx - lx) for (i, d), cx, lx in zip(ch, cl, logw)]
        rhat = [r[i] * jnp.exp(cx) for (i, d), cx in zip(ch, cl)]
        bk = each(lambda b, kx, e: jnp.concatenate([stack(b * e), stack(kx * e)], axis=0), bt, kd, e_neg)
        bkc = each(lambda b, kx, e: jnp.concatenate([b * e, kx * e], axis=0).astype(BF16), bt, kd, e_tot)
        big = [jnp.where(big_mask[d], _dot_nt(jnp.concatenate([x, y], axis=0).astype(BF16), z), 0.0)
               for (i, d), x, y, z in zip(ch, ahat, rhat, bk)]
        n1 = [b[0:C, 0:LANES] for b in big]
        n1b, n1s = bf(n1), each(stack, n1)
        xyv = [_dot(b[:, LANES:].astype(BF16), vs[i]) for (i, d), b in zip(ch, big)]
        n2 = each(_dot, n1b, n1s)
        n2b, n2s = bf(n2), each(stack, n2)
        n4 = each(_dot, n2b, n2s)
        p1 = each(lambda x1, x2, x1b, x2s: eye + x1 + x2 + _dot(x1b, x2s), n1, n2, n1b, n2s)
        n4b, n4s = bf(n4), each(stack, n4)
        n8 = each(_dot, n4b, n4s)
        n8b, n8s = bf(n8), each(stack, n8)
        n16 = each(_dot, n8b, n8s)
        p2 = each(lambda x1, x2, x1b, x2s: eye + x1 + x2 + _dot(x1b, x2s), n4, n8, n4b, n8s)
        n16b, n16s = bf(n16), each(stack, n16)
        n32 = each(_dot, n16b, n16s)
        p12 = each(_dot, bf(p1), each(stack, p2))
        p3 = each(lambda x1, x2, x1b, x2s: eye + x1 + x2 + _dot(x1b, x2s), n16, n32, n16b, each(stack, n32))
        tb = bf(each(_dot, bf(p12), each(stack, p3)))
        au = each(lambda t, x, y: _dot(t, jnp.concatenate([stack(x), stack(y[0:C])], axis=1)),
                  tb, ahat, xyv)
        qy = each(lambda b, x: _dot(b[C:, 0:LANES].astype(BF16),
                                    jnp.concatenate([stack(x[:, 0:LANES]), stack(x[:, LANES:])], axis=1)),
                  big, au)
        aub = bf(au)
        lr = each(lambda x, y: _dot_tn(x[:, 0:LANES], y[0:C]), aub, bkc)
        g = [_dot_tn(jnp.concatenate([x[:, LANES:], v[i].astype(BF16)], axis=0), y)
             for (i, d), x, y in zip(ch, aub, bkc)]
        for (i, d), x, q, l, gx, t in zip(ch, rhat, qy, lr, g, ltot):
            q_s[d, cid[i]] = (x + q[:, 0:LANES]).astype(BF16)
            lr_s[d, cid[i]] = jnp.where(same_head, l, 0.0).astype(BF16)
            g_s[d, cid[i]] = jnp.where(same_head, gx, 0.0)
            gam_s[d, cid[i]] = jnp.broadcast_to(jnp.exp(t), (8, LANES))
        for i, f0 in enumerate(flat):
            y = bonus[i]
            for d in range(2):
                y = y + xyv[2 * i + d][C:] + qy[2 * i + d][:, LANES:]
            ybuf[pl.ds(f0, C), :] = y

    per_seq = unroll // nseq

    def group(i):
        return [(s, i * per_seq + u) for s in range(nseq) for u in range(per_seq)]

    def prep_body(i, carry):
        prep_group(group(i))
        return carry

    lax.fori_loop(0, n_chunks // per_seq, prep_body, 0)

    chains = [(s, d) for s in range(nseq) for d in range(2)]

    def step_body(i, carry):
        cs = (i, n_chunks - 1 - i)
        cid = [s * n_chunks + cs[d] for s, d in chains]
        sb = bf(carry)
        ys = [_dot_nt(q_s[d, c], x) for (s, d), c, x in zip(chains, cid, sb)]
        ls = [_dot(x, lr_s[d, c]) for (s, d), c, x in zip(chains, cid, sb)]
        for (s, d), y in zip(chains, ys):
            f0 = pl.multiple_of(s * seq + cs[d] * C, C)
            ybuf[pl.ds(f0, C), :] += y
        return tuple(x * gam_s[d, c][0:1] + l + g_s[d, c] for (s, d), c, x, l in zip(chains, cid, carry, ls))

    zero_h = jnp.zeros((HEAD_SIZE, HEAD_SIZE), F32)

    def pair_state(s2):
        return jnp.concatenate([jnp.concatenate([s2[0], zero_h], axis=1),
                                jnp.concatenate([zero_h, s2[1]], axis=1)], axis=0)

    if has_s0:
        init = tuple(pair_state(s0_ref[s, d]) for s, d in chains)
    else:
        init = (jnp.zeros((LANES, LANES), F32),) * len(chains)
    fin = lax.fori_loop(0, n_chunks, step_body, init)
    if want_fin:
        for (s, d), x in zip(chains, fin):
            sfin_ref[s, d, 0] = x[0:HEAD_SIZE, 0:HEAD_SIZE]
            sfin_ref[s, d, 1] = x[HEAD_SIZE:, HEAD_SIZE:]

    def out_body(i, carry):
        grp = group(i)
        rows = [pl.multiple_of(c * C, C) for s, c in grp]
        flat = [pl.multiple_of(s * seq + c * C, C) for s, c in grp]
        y = [ybuf[pl.ds(f0, C), :] for f0 in flat]
        g = [_dot(loa_s[pl.ds(f0, C), LANES:LORA_OUT], wgt_ref[0]) for f0 in flat]
        mean = each(lambda x: seg_sum(x) * (1.0 / HEAD_SIZE), y)
        dlt = each(lambda x, m: x - m, y, mean)
        var = each(lambda x: seg_sum(x * x) * (1.0 / HEAD_SIZE), dlt)
        for (s, c), r0, dx, vx, gx in zip(grp, rows, dlt, var, g):
            yn = dx * lax.rsqrt(vx + GN_EPS) * vec[_V_LNW:_V_LNW + 1] + vec[_V_LNB:_V_LNB + 1]
            out_ref[s, pl.ds(r0, C), :] = (yn * gx).astype(BF16)
        return carry

    lax.fori_loop(0, n_chunks // per_seq, out_body, 0)


def _wkv(proj3, mu3, mu_lora, ww, wa, wgt, vec, s0, grid_mode, want_fin):
    nb, seq, _ = proj3.shape
    n_chunks = seq // CHUNK
    has_s0 = s0 is not None
    group_chunks = 8
    nseq = 2 if (2 * n_chunks <= group_chunks and nb % 2 == 0) else 1
    kern = functools.partial(_wkv_kernel, seq=seq, nseq=nseq, unroll=min(group_chunks, nseq * n_chunks),
                             grid_mode=grid_mode, has_s0=has_s0, want_fin=want_fin)
    state_spec = pl.BlockSpec((nseq, 2, 2, HEAD_SIZE, HEAD_SIZE), lambda b, j: (b, 0, j, 0, 0))
    in_specs = [pl.BlockSpec((nseq, seq, LANES), lambda b, j: (b, 0, R_BLK + j)),
                pl.BlockSpec((nseq, seq, LANES), lambda b, j: (b, 0, K_BLK + j)),
                pl.BlockSpec((nseq, seq, LANES), lambda b, j: (b, 0, V_BLK + j)),
                pl.BlockSpec((nseq, seq, LORA_OUT), lambda b, j: (b, 0, 0)),
                pl.BlockSpec((1, 3, LANES), lambda b, j: (j, 0, 0)),
                pl.BlockSpec((1, LORA_OUT), lambda b, j: (0, 0)),
                pl.BlockSpec((1, LANES, 2 * LANES), lambda b, j: (j, 0, 0)),
                pl.BlockSpec((1, LANES, 2 * LANES), lambda b, j: (j, 0, 0)),
                pl.BlockSpec((1, 2 * LANES, LANES), lambda b, j: (j, 0, 0)),
                pl.BlockSpec((1, 16, LANES), lambda b, j: (j, 0, 0))]
    args = [proj3, proj3, proj3, proj3, mu3, mu_lora, ww, wa, wgt, vec]
    out_specs = [pl.BlockSpec((nseq, seq, LANES), lambda b, j: (b, 0, j))]
    out_shape = [jax.ShapeDtypeStruct((nb, seq, RWKV_WIDTH), BF16)]
    if has_s0:
        in_specs.append(state_spec)
        args.append(s0)
    if want_fin:
        out_specs.append(state_spec)
        out_shape.append(jax.ShapeDtypeStruct((nb, 2, RWKV_HEADS, HEAD_SIZE, HEAD_SIZE), F32))
    slots = nseq * n_chunks
    res = pl.pallas_call(
        kern,
        grid=(nb // nseq, PAIRS),
        in_specs=in_specs,
        out_specs=out_specs,
        out_shape=out_shape,
        scratch_shapes=[pltpu.VMEM((2, slots, CHUNK, LANES), BF16),
                        pltpu.VMEM((2, slots, LANES, LANES), BF16),
                        pltpu.VMEM((2, slots, LANES, LANES), F32),
                        pltpu.VMEM((2, slots, 8, LANES), F32),
                        pltpu.VMEM((nseq * seq, LANES), F32),
                        pltpu.VMEM((nseq * seq, LORA_OUT), BF16)],
        compiler_params=_cparams(("arbitrary", "arbitrary")),
        name="wkv",
    )(*args)
    return res[0], (res[1] if want_fin else None)


def _out_proj_kernel(lru_ref, rw_ref, w1_ref, w2_ref, x_ref, mod_ref, g_ref, x1_ref, h2_ref):
    out = _dot(lru_ref[...], w1_ref[...]) + _dot(rw_ref[...], w2_ref[...])
    m = mod_ref[0]
    g = g_ref[...]
    x1 = x_ref[...] + m[2:3] * _rms(out, g[0:1])
    x1_ref[...] = x1
    h2_ref[...] = (_rms(x1, g[1:2]) * (1.0 + m[4:5]) + m[3:4]).astype(BF16)


def _out_proj(lru_o, rw_o, w_out_b, x2d, mod3, g2, mod_idx, tm=512):
    m = x2d.shape[0]
    half = D_MODEL // 2
    return pl.pallas_call(
        _out_proj_kernel,
        grid=(m // tm,),
        in_specs=[pl.BlockSpec((tm, half), lambda i: (i, 0)),
                  pl.BlockSpec((tm, half), lambda i: (i, 0)),
                  pl.BlockSpec((half, D_MODEL), lambda i: (0, 0)),
                  pl.BlockSpec((half, D_MODEL), lambda i: (1, 0)),
                  pl.BlockSpec((tm, D_MODEL), lambda i: (i, 0)),
                  pl.BlockSpec((1, 6, D_MODEL), lambda i: (mod_idx(i * tm), 0, 0)),
                  pl.BlockSpec((2, D_MODEL), lambda i: (0, 0))],
        out_specs=[pl.BlockSpec((tm, D_MODEL), lambda i: (i, 0)),
                   pl.BlockSpec((tm, D_MODEL), lambda i: (i, 0))],
        out_shape=[jax.ShapeDtypeStruct((m, D_MODEL), F32),
                   jax.ShapeDtypeStruct((m, D_MODEL), BF16)],
        compiler_params=_cparams(("parallel",)),
        name="out_proj",
    )(lru_o, rw_o, w_out_b, w_out_b, x2d, mod3, g2)


def _ffn_kernel(h_ref, wg_ref, wu_ref, wd_ref, x1_ref, mod_ref, g_ref, o_ref, acc):
    f = pl.program_id(1)
    h = h_ref[...]
    gate = _dot(h, wg_ref[...])
    up = _dot(h, wu_ref[...])
    act = (gate * jax.nn.sigmoid(gate) * up).astype(BF16)
    part = _dot(act, wd_ref[...])

    @pl.when(f == 0)
    def _():
        acc[...] = part

    @pl.when(f > 0)
    def _():
        acc[...] += part

    @pl.when(f == pl.num_programs(1) - 1)
    def _():
        o_ref[...] = x1_ref[...] + mod_ref[0][5:6] * _rms(acc[...], g_ref[...])


def _ffn(h2, w_gu_b, w_down_b, x1, mod3, g, mod_idx, tm=512, tf=512):
    m = h2.shape[0]
    nf = FFN_HIDDEN // tf
    return pl.pallas_call(
        _ffn_kernel,
        grid=(m // tm, nf),
        in_specs=[pl.BlockSpec((tm, D_MODEL), lambda i, f: (i, 0)),
                  pl.BlockSpec((D_MODEL, tf), lambda i, f: (0, f)),
                  pl.BlockSpec((D_MODEL, tf), lambda i, f: (0, nf + f)),
                  pl.BlockSpec((tf, D_MODEL), lambda i, f: (f, 0)),
                  pl.BlockSpec((tm, D_MODEL), lambda i, f: (i, 0)),
                  pl.BlockSpec((1, 6, D_MODEL), lambda i, f: (mod_idx(i * tm), 0, 0)),
                  pl.BlockSpec((1, D_MODEL), lambda i, f: (0, 0))],
        out_specs=pl.BlockSpec((tm, D_MODEL), lambda i, f: (i, 0)),
        out_shape=jax.ShapeDtypeStruct((m, D_MODEL), F32),
        scratch_shapes=[pltpu.VMEM((tm, D_MODEL), F32)],
        compiler_params=_cparams(("parallel", "arbitrary")),
        name="ffn",
    )(h2, w_gu_b, w_gu_b, w_down_b, x1, mod3, g)


def _group(x3, mod3, mod_idx, grid_mode, lru_h0, wkv_s0, want_fin, P):
    nb, seq, _ = x3.shape
    x2d = x3.reshape(nb * seq, D_MODEL)
    proj = _in_proj(x2d, mod3, P["norm_mix_pre"], P["w_in"], mod_idx).reshape(nb, seq, PROJ_PAD)
    lru_o, lru_fin = _lru(proj, P["conv_w"], P["conv_b"], P["lru_wg"], P["lru_bg"], P["lru_lam"], lru_h0)
    rw_o, s_fin = _wkv(proj, P["mu3"], P["mu_lora"], P["ww"], P["wa"], P["wgt"], P["vec"], wkv_s0,
                       grid_mode, want_fin)
    x1, h2 = _out_proj(lru_o.reshape(nb * seq, LRU_WIDTH), rw_o.reshape(nb * seq, RWKV_WIDTH),
                       P["w_out"], x2d, mod3, P["g_post_pre"], mod_idx)
    y = _ffn(h2, P["w_gu"], P["w_down"], x1, mod3, P["norm_ffn_post"], mod_idx)
    return y.reshape(nb, seq, D_MODEL), lru_fin, s_fin


def _prep_params(norm_mix_pre, norm_mix_post, norm_ffn_pre, norm_ffn_post, w_in,
                 lru_conv_w, lru_conv_b, lru_wr, lru_br, lru_wi, lru_bi, lru_lambda,
                 rwkv_mu, rwkv_w0, rwkv_w_up, rwkv_a0, rwkv_a_up, rwkv_g_up, rwkv_k_k, rwkv_k_a, rwkv_r_k,
                 rwkv_ln_w, rwkv_ln_b, w_out, ffn_w_gu, ffn_w_down):
    P = {}
    P["norm_mix_pre"] = norm_mix_pre.reshape(1, D_MODEL)
    P["g_post_pre"] = jnp.stack([norm_mix_post, norm_ffn_pre], axis=0)
    P["norm_ffn_post"] = norm_ffn_post.reshape(1, D_MODEL)
    w_in_b = w_in.astype(BF16)
    n_main = IN_WIDTH - LORA_WIDTH
    P["w_in"] = jnp.concatenate([w_in_b[:, n_main:], jnp.zeros((D_MODEL, LORA_PAD - LORA_WIDTH), BF16),
                                 w_in_b[:, :n_main]], axis=1)
    P["conv_w"] = lru_conv_w
    P["conv_b"] = lru_conv_b.reshape(1, LRU_WIDTH)
    P["lru_wg"] = jnp.concatenate([lru_wr[0], lru_wi[0], lru_wr[1], lru_wi[1]], axis=-1).astype(BF16)
    P["lru_bg"] = jnp.concatenate([lru_br[0], lru_bi[0], lru_br[1], lru_bi[1]], axis=-1).reshape(LRU_HEADS, 1, 512)
    P["lru_lam"] = jnp.swapaxes(lru_lambda.reshape(2, LRU_HEADS, LRU_HEAD_DIM), 0, 1)
    P["mu3"] = jnp.swapaxes(rwkv_mu[:3 * RWKV_WIDTH].reshape(3, PAIRS, LANES), 0, 1)
    P["mu_lora"] = jnp.pad(rwkv_mu[3 * RWKV_WIDTH:], (0, LORA_OUT - LORA_WIDTH)).reshape(1, LORA_OUT)

    def per_pair(w):
        return jnp.swapaxes(w.reshape(w.shape[0], PAIRS, LANES), 0, 1)

    wu = jnp.concatenate([per_pair(rwkv_w_up[0]), per_pair(rwkv_w_up[1])], axis=-1)
    au = jnp.concatenate([per_pair(rwkv_a_up[0]), per_pair(rwkv_a_up[1])], axis=-1)
    P["ww"] = jnp.pad(wu, ((0, 0), (0, LANES - DECAY_LORA), (0, 0))).astype(BF16)
    P["wa"] = jnp.pad(au, ((0, 0), (DECAY_LORA, LANES - DECAY_LORA - AAA_LORA), (0, 0))).astype(BF16)
    P["wgt"] = jnp.pad(per_pair(rwkv_g_up), ((0, 0), (0, 2 * LANES - GATE_LORA), (0, 0))).astype(BF16)
    rows = [rwkv_w0[0], rwkv_w0[1], rwkv_a0[0], rwkv_a0[1], rwkv_k_k, rwkv_k_a,
            rwkv_r_k.reshape(RWKV_WIDTH), rwkv_ln_w, rwkv_ln_b]
    vec = jnp.stack(rows + [jnp.zeros_like(rwkv_k_k)] * (16 - len(rows)), axis=0)
    P["vec"] = per_pair(vec)
    P["w_out"] = w_out.astype(BF16)
    P["w_gu"] = ffn_w_gu.astype(BF16)
    P["w_down"] = ffn_w_down.astype(BF16)
    return P


def kernel(x_prompt, x_sample, state_lru, state_wkv, c, c_ctx, norm_mix_pre, norm_mix_post, norm_ffn_pre,
           norm_ffn_post, w_mod, b_mod, w_in, lru_conv_w, lru_conv_b, lru_wr, lru_br, lru_wi, lru_bi,
           lru_lambda, rwkv_mu, rwkv_w0, rwkv_w_up, rwkv_a0, rwkv_a_up, rwkv_g_up, rwkv_k_k, rwkv_k_a,
           rwkv_r_k, rwkv_ln_w, rwkv_ln_b, w_out, ffn_w_gu, ffn_w_down):
    depth = w_in.shape[0]
    nb_p = x_prompt.shape[0]
    nb_s, seq_s, _ = x_sample.shape
    y_p, y_s = x_prompt, x_sample
    new_lru, new_wkv = [], []
    cvec = jnp.concatenate([c_ctx[None, :], c, jnp.zeros((8 - 1 - nb_s, D_MODEL), F32)], axis=0)
    for l in range(depth):
        P = _prep_params(norm_mix_pre[l], norm_mix_post[l], norm_ffn_pre[l], norm_ffn_post[l], w_in[l],
                         lru_conv_w[l], lru_conv_b[l], lru_wr[l], lru_br[l], lru_wi[l], lru_bi[l],
                         lru_lambda[l], rwkv_mu[l], rwkv_w0[l], rwkv_w_up[l], rwkv_a0[l], rwkv_a_up[l],
                         rwkv_g_up[l], rwkv_k_k[l], rwkv_k_a[l], rwkv_r_k[l], rwkv_ln_w[l], rwkv_ln_b[l],
                         w_out[l], ffn_w_gu[l], ffn_w_down[l])
        mod3 = _mod(cvec, w_mod[l], b_mod[l].reshape(1, -1)).reshape(8, 6, D_MODEL)
        y_p, lru_ctx, wkv_ctx = _group(
            y_p, mod3, lambda row: 0, False, jnp.zeros((nb_p, 2, LRU_WIDTH), F32), None, True, P)
        new_lru.append(lru_ctx)
        new_wkv.append(wkv_ctx)
        y_s, _, _ = _group(
            y_s, mod3, lambda row: 1 + row // seq_s, True, state_lru[:, l], state_wkv[:, l], False, P)
    return (y_p, y_s, jnp.stack(new_lru, axis=1), jnp.stack(new_wkv, axis=1))
```

```python
import functools

import jax
import jax.numpy as jnp
from jax import lax
from jax.experimental import pallas as pl
from jax.experimental.pallas import tpu as pltpu

F32 = jnp.float32
BF16 = jnp.bfloat16

D_MODEL = 2048
LRU_WIDTH = 1024
LRU_HEADS = 8
LRU_HEAD_DIM = 128
LRU_C = 8.0
RWKV_WIDTH = 1024
HEAD_SIZE = 64
RWKV_HEADS = 16
DECAY_LORA = 64
AAA_LORA = 64
GATE_LORA = 160
LORA_WIDTH = DECAY_LORA + AAA_LORA + GATE_LORA
RWKV_IN_WIDTH = 3 * RWKV_WIDTH + LORA_WIDTH
IN_WIDTH = 2 * LRU_WIDTH + RWKV_IN_WIDTH
FFN_HIDDEN = 5632
GRID_W = 64
RMS_EPS = 1e-6
GN_EPS = 64e-5

LANES = 128
LORA_OUT = 384
LORA_PAD = 512
PROJ_PAD = LORA_PAD + 2 * LRU_WIDTH + 3 * RWKV_WIDTH
XL_BLK = LORA_PAD // 128
GL_BLK = XL_BLK + LRU_WIDTH // 128
R_BLK = GL_BLK + LRU_WIDTH // 128
K_BLK = R_BLK + RWKV_WIDTH // 128
V_BLK = K_BLK + RWKV_WIDTH // 128
CHUNK = 64
PAIRS = RWKV_HEADS // 2
VMEM_LIMIT = 56 * 1024 * 1024


def _cparams(sem, flags=None):
    return pltpu.CompilerParams(dimension_semantics=sem, vmem_limit_bytes=VMEM_LIMIT, flags=flags)


def _dot(a, b):
    return jnp.dot(a, b, preferred_element_type=F32)


def _dot_nt(a, b):
    return lax.dot_general(a, b, (((1,), (1,)), ((), ())), preferred_element_type=F32)


def _dot_tn(a, b):
    return lax.dot_general(a, b, (((0,), (0,)), ((), ())), preferred_element_type=F32)


def _split3(x):
    hi = x.astype(BF16)
    r1 = x - hi.astype(F32)
    mid = r1.astype(BF16)
    lo = (r1 - mid.astype(F32)).astype(BF16)
    return hi, mid, lo


def _softplus(x):
    return jnp.maximum(x, 0.0) + jnp.log1p(jnp.exp(-jnp.abs(x)))


def _rms(x, g):
    return x * lax.rsqrt(jnp.mean(x * x, axis=-1, keepdims=True) + RMS_EPS) * g


def _mod_kernel(c_ref, w_ref, b_ref, o_ref):
    c = c_ref[...]
    s = c * jax.nn.sigmoid(c)
    o_ref[...] = jnp.dot(s, w_ref[...], precision=lax.Precision.HIGHEST,
                         preferred_element_type=F32) + b_ref[...]


def _mod(cvec, w_mod, b_mod):
    n = w_mod.shape[1]
    tn = 1024
    return pl.pallas_call(
        _mod_kernel,
        grid=(n // tn,),
        in_specs=[pl.BlockSpec((8, D_MODEL), lambda j: (0, 0)),
                  pl.BlockSpec((D_MODEL, tn), lambda j: (0, j)),
                  pl.BlockSpec((1, tn), lambda j: (0, j))],
        out_specs=pl.BlockSpec((8, tn), lambda j: (0, j)),
        out_shape=jax.ShapeDtypeStruct((8, n), F32),
        compiler_params=_cparams(("parallel",)),
        name="mod",
    )(cvec, w_mod, b_mod)


def _in_proj_kernel(x_ref, mod_ref, g_ref, w_ref, o_ref, h_scr):
    @pl.when(pl.program_id(1) == 0)
    def _():
        m = mod_ref[0]
        h = _rms(x_ref[...], g_ref[...]) * (1.0 + m[1:2]) + m[0:1]
        h_scr[...] = h.astype(BF16)

    o_ref[...] = _dot(h_scr[...], w_ref[...])


def _in_proj(x2d, mod3, g, w_in_p, mod_idx, tm=1024, tn=1408):
    m = x2d.shape[0]
    return pl.pallas_call(
        _in_proj_kernel,
        grid=(m // tm, PROJ_PAD // tn),
        in_specs=[pl.BlockSpec((tm, D_MODEL), lambda i, j: (i, 0)),
                  pl.BlockSpec((1, 6, D_MODEL), lambda i, j: (mod_idx(i * tm), 0, 0)),
                  pl.BlockSpec((1, D_MODEL), lambda i, j: (0, 0)),
                  pl.BlockSpec((D_MODEL, tn), lambda i, j: (0, j))],
        out_specs=pl.BlockSpec((tm, tn), lambda i, j: (i, j)),
        out_shape=jax.ShapeDtypeStruct((m, PROJ_PAD), F32),
        scratch_shapes=[pltpu.VMEM((tm, D_MODEL), BF16)],
        compiler_params=_cparams(("parallel", "arbitrary")),
        name="in_proj",
    )(x2d, mod3, g, w_in_p)


def _lru_kernel(xl_ref, gl_ref, cw_ref, cb_ref, wg_ref, bg_ref, lam_ref, h0_ref,
                out_ref, hfin_ref, xpad, a_f, b_f, a_b, b_b, *, seq, tile):
    n_tiles = seq // tile
    zeros8 = jnp.zeros((8, LANES), F32)
    xpad[pl.ds(0, 8), :] = zeros8
    xpad[pl.ds(seq + 8, 8), :] = zeros8

    def copy_body(i, c):
        r0 = pl.multiple_of(i * tile, tile)
        xpad[pl.ds(r0 + 8, tile), :] = xl_ref[0, pl.ds(r0, tile), :]
        return c

    lax.fori_loop(0, n_tiles, copy_body, 0)

    half_c = -0.5 * LRU_C * _softplus(-lam_ref[0])
    cw = cw_ref[...]
    cb = cb_ref[...]
    wg_half = (0.5 * wg_ref[0].astype(F32)).astype(BF16)
    bg_half = 0.5 * bg_ref[0]

    def gate_body(i, c):
        r0 = pl.multiple_of(i * tile, tile)
        ext = xpad[pl.ds(r0, tile + 16), :]
        n_ext = tile + 16
        xc = (pltpu.roll(ext, 2, 0)[8:8 + tile] * cw[0:1]
              + pltpu.roll(ext, 1, 0)[8:8 + tile] * cw[1:2]
              + ext[8:8 + tile] * cw[2:3]
              + pltpu.roll(ext, n_ext - 1, 0)[8:8 + tile] * cw[3:4]) + cb
        t = jnp.tanh(_dot(xc.astype(BF16), wg_half) + bg_half)
        half_x = 0.5 * xc
        for d, (a_s, b_s) in enumerate(((a_f, b_f), (a_b, b_b))):
            log_a = half_c[d:d + 1] * t[:, 256 * d:256 * d + 128] + half_c[d:d + 1]
            a = jnp.exp(log_a)
            u = jnp.tanh(log_a) * (-a * a - 1.0)
            ix = half_x * t[:, 256 * d + 128:256 * d + 256] + half_x
            a_s[pl.ds(r0, tile), :] = a
            b_s[pl.ds(r0, tile), :] = jnp.where(u > 0.0, u * lax.rsqrt(u), 0.0) * ix
        return c

    lax.fori_loop(0, n_tiles, gate_body, 0)

    group = 4
    n_steps = seq // (8 * group)
    rid = lax.broadcasted_iota(jnp.int32, (8, LANES), 0)

    def tile_scan(a, b, reverse):
        for s in (1, 2, 4):
            if reverse:
                keep = rid < 8 - s
                a_sh = jnp.where(keep, pltpu.roll(a, 8 - s, 0), 1.0)
                b_sh = jnp.where(keep, pltpu.roll(b, 8 - s, 0), 0.0)
            else:
                keep = rid >= s
                a_sh = jnp.where(keep, pltpu.roll(a, s, 0), 1.0)
                b_sh = jnp.where(keep, pltpu.roll(b, s, 0), 0.0)
            b = a * b_sh + b
            a = a * a_sh
        return a, b

    def scan_body(i, carry):
        hf, hb = carry
        rf = [pl.multiple_of((i * group + u) * 8, 8) for u in range(group)]
        rb = [pl.multiple_of(((n_steps - 1 - i) * group + (group - 1 - u)) * 8, 8) for u in range(group)]
        sf = [tile_scan(a_f[pl.ds(r, 8), :], b_f[pl.ds(r, 8), :], False) for r in rf]
        sb = [tile_scan(a_b[pl.ds(r, 8), :], b_b[pl.ds(r, 8), :], True) for r in rb]
        for u in range(group):
            h = sf[u][1] + sf[u][0] * hf
            b_f[pl.ds(rf[u], 8), :] = h
            hf = h[7:8]
            h = sb[u][1] + sb[u][0] * hb
            b_b[pl.ds(rb[u], 8), :] = h
            hb = h[0:1]
        return hf, hb

    h0 = h0_ref[0]
    hf, hb = lax.fori_loop(0, n_steps, scan_body, (h0[0:1], h0[1:2]))
    hfin_ref[0] = jnp.concatenate([hf, hb], axis=0)

    def out_body(i, c):
        r0 = pl.multiple_of(i * tile, tile)
        gl = gl_ref[0, pl.ds(r0, tile), :]
        hs = b_f[pl.ds(r0, tile), :] + b_b[pl.ds(r0, tile), :]
        out_ref[0, pl.ds(r0, tile), :] = (hs * jax.nn.gelu(gl)).astype(BF16)
        return c

    lax.fori_loop(0, n_tiles, out_body, 0)


def _lru(proj3, conv_w, conv_b, wg, bg, lam, h0):
    nb, seq, _ = proj3.shape
    tile = 256
    kern = functools.partial(_lru_kernel, seq=seq, tile=tile)
    return pl.pallas_call(
        kern,
        grid=(nb, LRU_HEADS),
        in_specs=[pl.BlockSpec((1, seq, LANES), lambda b, h: (b, 0, XL_BLK + h)),
                  pl.BlockSpec((1, seq, LANES), lambda b, h: (b, 0, GL_BLK + h)),
                  pl.BlockSpec((4, LANES), lambda b, h: (0, h)),
                  pl.BlockSpec((1, LANES), lambda b, h: (0, h)),
                  pl.BlockSpec((1, LANES, 512), lambda b, h: (h, 0, 0)),
                  pl.BlockSpec((1, 1, 512), lambda b, h: (h, 0, 0)),
                  pl.BlockSpec((1, 2, LANES), lambda b, h: (h, 0, 0)),
                  pl.BlockSpec((1, 2, LANES), lambda b, h: (b, 0, h))],
        out_specs=[pl.BlockSpec((1, seq, LANES), lambda b, h: (b, 0, h)),
                   pl.BlockSpec((1, 2, LANES), lambda b, h: (b, 0, h))],
        out_shape=[jax.ShapeDtypeStruct((nb, seq, LRU_WIDTH), BF16),
                   jax.ShapeDtypeStruct((nb, 2, LRU_WIDTH), F32)],
        scratch_shapes=[pltpu.VMEM((seq + 16, LANES), F32)] + [pltpu.VMEM((seq, LANES), F32)] * 4,
        compiler_params=_cparams(("parallel", "arbitrary")),
        name="lru",
    )(proj3, proj3, conv_w, conv_b, wg, bg, lam, h0)


def _shift_kinds(ch_lo, ch_hi, grid_mode):
    if grid_mode:
        q = RWKV_IN_WIDTH // 4
        parts = [("left", q), ("right", 2 * q), ("up", 3 * q), ("down", RWKV_IN_WIDTH)]
    else:
        parts = [("prev", RWKV_IN_WIDTH // 2), ("next", RWKV_IN_WIDTH)]
    kinds, lo = [], 0
    for kind, hi in parts:
        if ch_lo < hi and ch_hi > lo:
            kinds.append((kind, hi))
        lo = hi
    return kinds


def _shifted(x_ref, r0, seq, kind):
    C = CHUNK
    cur = x_ref[pl.ds(r0, C), :]
    rid = lax.broadcasted_iota(jnp.int32, cur.shape, 0)
    if kind == "left":
        return jnp.where(rid == 0, 0.0, pltpu.roll(cur, 1, 0))
    if kind == "right":
        return jnp.where(rid == C - 1, 0.0, pltpu.roll(cur, C - 1, 0))
    if kind == "up":
        src = pl.multiple_of(jnp.maximum(r0 - C, 0), C)
        return jnp.where(r0 > 0, x_ref[pl.ds(src, C), :], 0.0)
    if kind == "down":
        src = pl.multiple_of(jnp.minimum(r0 + C, seq - C), C)
        return jnp.where(r0 < seq - C, x_ref[pl.ds(src, C), :], 0.0)
    if kind == "prev":
        src = pl.multiple_of(jnp.maximum(r0 - 8, 0), 8)
        ext = jnp.concatenate([x_ref[pl.ds(src, 8), :], cur], axis=0)
        first = jnp.where(r0 == 0, 0, -1)
        return jnp.where(rid == first, 0.0, pltpu.roll(ext, 1, 0)[8:8 + C])
    assert kind == "next"
    src = pl.multiple_of(jnp.minimum(r0 + C, seq - 8), 8)
    ext = jnp.concatenate([cur, x_ref[pl.ds(src, 8), :]], axis=0)
    last = jnp.where(r0 == seq - C, C - 1, -1)
    return jnp.where(rid == last, 0.0, pltpu.roll(ext, C + 7, 0)[0:C])


def _token_shift(x_ref, r0, seq, mu, ch0, ch_lo, ch_hi, grid_mode):
    cur = x_ref[pl.ds(r0, CHUNK), :]
    kinds = _shift_kinds(ch_lo, ch_hi, grid_mode)
    sh = _shifted(x_ref, r0, seq, kinds[-1][0])
    if len(kinds) > 1:
        cg = ch0 + lax.broadcasted_iota(jnp.int32, cur.shape, 1)
        for kind, hi in reversed(kinds[:-1]):
            sh = jnp.where(cg < hi, _shifted(x_ref, r0, seq, kind), sh)
    return cur + mu * (sh - cur)


_V_W0, _V_A0, _V_KK, _V_KA, _V_RK, _V_LNW, _V_LNB = 0, 2, 4, 5, 6, 7, 8


def _wkv_kernel(*refs, seq, nseq, unroll, grid_mode, has_s0, want_fin):
    it = iter(refs)
    r_ref, k_ref, v_ref, lo_ref, mu_ref, mul_ref, ww_ref, wa_ref, wgt_ref, vec_ref = (next(it) for _ in range(10))
    s0_ref = next(it) if has_s0 else None
    out_ref = next(it)
    sfin_ref = next(it) if want_fin else None
    q_s, lr_s, g_s, gam_s, ybuf, loa_s = it
    n_chunks = seq // CHUNK
    C = CHUNK
    pair = pl.program_id(1)
    vec = vec_ref[0]
    mu = mu_ref[0]

    @pl.when(pair == 0)
    def _():
        def lora_body(i, carry):
            r0 = pl.multiple_of(i * C, C)
            for s in range(nseq):
                y = _token_shift(lo_ref.at[s], r0, seq, mul_ref[...], 3 * RWKV_WIDTH, 3 * RWKV_WIDTH,
                                 RWKV_IN_WIDTH, grid_mode)
                ln = lax.broadcasted_iota(jnp.int32, y.shape, 1)
                act = jnp.where(ln < DECAY_LORA, jnp.tanh(y),
                                jnp.where(ln < DECAY_LORA + AAA_LORA, y, jax.nn.sigmoid(y)))
                loa_s[pl.ds(s * seq + r0, C), :] = act.astype(BF16)
            return carry

        lax.fori_loop(0, n_chunks, lora_body, 0)

    lane = lax.broadcasted_iota(jnp.int32, (C, LANES), 1)
    row = lax.broadcasted_iota(jnp.int32, (C, LANES), 0)
    head0 = lane < HEAD_SIZE
    eye = (lane % C == row).astype(F32)
    ri = lax.broadcasted_iota(jnp.int32, (LANES, LANES), 0)
    ci = lax.broadcasted_iota(jnp.int32, (LANES, LANES), 1)
    same_head = (ri // HEAD_SIZE) == (ci // HEAD_SIZE)
    seg2 = jnp.concatenate([same_head.astype(BF16)] * 2, axis=0)
    tr = lax.broadcasted_iota(jnp.int32, (C, 3 * C), 0)
    tc = lax.broadcasted_iota(jnp.int32, (C, 3 * C), 1) % C
    tri3 = ((tc <= tr).astype(BF16), (tc >= tr).astype(BF16))
    br = lax.broadcasted_iota(jnp.int32, (2 * C, 2 * LANES), 0)
    bc = lax.broadcasted_iota(jnp.int32, (2 * C, 2 * LANES), 1)
    brt, bct = br % C, bc % C
    incl = (br >= C).astype(jnp.int32)
    big_mask = (bct < brt + incl, bct > brt - incl)

    def stack(x):
        return jnp.concatenate([jnp.where(head0, x, 0.0), jnp.where(head0, 0.0, x)], axis=0).astype(BF16)

    def seg_sum(x):
        hi = x.astype(BF16)
        lo = (x - hi.astype(F32)).astype(BF16)
        return _dot(jnp.concatenate([hi, lo], axis=1), seg2)

    def cum_sum(d, x):
        return _dot(tri3[d], jnp.concatenate(_split3(x), axis=0))

    def each(fn, *lists):
        return [fn(*xs) for xs in zip(*lists)]

    def bf(xs):
        return [x.astype(BF16) for x in xs]

    def prep_group(chunks):
        seqs = [s for s, c in chunks]
        rows = [pl.multiple_of(c * C, C) for s, c in chunks]
        flat = [pl.multiple_of(s * seq + c * C, C) for s, c in chunks]
        cid = [s * n_chunks + c for s, c in chunks]
        ch0 = pair * LANES
        r = [_token_shift(r_ref.at[s], r0, seq, mu[0:1], ch0, 0, RWKV_WIDTH, grid_mode) for s, r0 in zip(seqs, rows)]
        k = [_token_shift(k_ref.at[s], r0, seq, mu[1:2], RWKV_WIDTH + ch0, RWKV_WIDTH, 2 * RWKV_WIDTH, grid_mode)
             for s, r0 in zip(seqs, rows)]
        v = [_token_shift(v_ref.at[s], r0, seq, mu[2:3], 2 * RWKV_WIDTH + ch0, 2 * RWKV_WIDTH, 3 * RWKV_WIDTH,
                          grid_mode) for s, r0 in zip(seqs, rows)]
        lo = [loa_s[pl.ds(f0, C), 0:LANES] for f0 in flat]
        lw = each(lambda x: _dot(x, ww_ref[0]), lo)
        la = each(lambda x: _dot(x, wa_ref[0]), lo)
        kk = each(lambda x: x * vec[_V_KK:_V_KK + 1], k)
        ss = each(lambda x: seg_sum(x * x), kk)
        kk = each(lambda x, s: x * lax.rsqrt(jnp.maximum(s, 1e-24)), kk, ss)
        bonus = each(lambda rr, kx, vx: seg_sum(rr * kx * vec[_V_RK:_V_RK + 1]) * vx, r, k, v)
        vs = each(stack, v)
        ch = [(i, d) for i in range(len(chunks)) for d in range(2)]
        logw = [-jnp.exp(-_softplus(-(vec[_V_W0 + d:_V_W0 + d + 1] + lw[i][:, d * LANES:(d + 1) * LANES])) - 0.5)
                for i, d in ch]
        a = [jax.nn.sigmoid(vec[_V_A0 + d:_V_A0 + d + 1] + la[i][:, d * LANES:(d + 1) * LANES]) for i, d in ch]
        kd = [k[i] * (1.0 + (ax - 1.0) * vec[_V_KA:_V_KA + 1]) for (i, d), ax in zip(ch, a)]
        bt = [kk[i] * ax for (i, d), ax in zip(ch, a)]
        cl = [cum_sum(d, lx) for (i, d), lx in zip(ch, logw)]
        ltot = each(lambda lx: jnp.sum(lx, axis=0, keepdims=True), logw)
        e_neg = each(lambda x: jnp.exp(-x), cl)
        e_tot = each(lambda t, x: jnp.exp(t - x), ltot, cl)
        ahat = [-kk[i] * jnp.exp(cx - lx) for (i, d), cx, lx in zip(ch, cl, logw)]
        rhat = [r[i] * jnp.exp(cx) for (i, d), cx in zip(ch, cl)]
        bk = each(lambda b, kx, e: jnp.concatenate([stack(b * e), stack(kx * e)], axis=0), bt, kd, e_neg)
        bkc = each(lambda b, kx, e: jnp.concatenate([b * e, kx * e], axis=0).astype(BF16), bt, kd, e_tot)
        big = [jnp.where(big_mask[d], _dot_nt(jnp.concatenate([x, y], axis=0).astype(BF16), z), 0.0)
               for (i, d), x, y, z in zip(ch, ahat, rhat, bk)]
        n1 = [b[0:C, 0:LANES] for b in big]
        n1b, n1s = bf(n1), each(stack, n1)
        xyv = [_dot(b[:, LANES:].astype(BF16), vs[i]) for (i, d), b in zip(ch, big)]
        n2 = each(_dot, n1b, n1s)
        n2b, n2s = bf(n2), each(stack, n2)
        n4 = each(_dot, n2b, n2s)
        p1 = each(lambda x1, x2, x1b, x2s: eye + x1 + x2 + _dot(x1b, x2s), n1, n2, n1b, n2s)
        n4b, n4s = bf(n4), each(stack, n4)
        n8 = each(_dot, n4b, n4s)
        n8b, n8s = bf(n8), each(stack, n8)
        n16 = each(_dot, n8b, n8s)
        p2 = each(lambda x1, x2, x1b, x2s: eye + x1 + x2 + _dot(x1b, x2s), n4, n8, n4b, n8s)
        n16b, n16s = bf(n16), each(stack, n16)
        n32 = each(_dot, n16b, n16s)
        p12 = each(_dot, bf(p1), each(stack, p2))
        p3 = each(lambda x1, x2, x1b, x2s: eye + x1 + x2 + _dot(x1b, x2s), n16, n32, n16b, each(stack, n32))
        tb = bf(each(_dot, bf(p12), each(stack, p3)))
        au = each(lambda t, x, y: _dot(t, jnp.concatenate([stack(x), stack(y[0:C])], axis=1)),
                  tb, ahat, xyv)
        qy = each(lambda b, x: _dot(b[C:, 0:LANES].astype(BF16),
                                    jnp.concatenate([stack(x[:, 0:LANES]), stack(x[:, LANES:])], axis=1)),
                  big, au)
        aub = bf(au)
        lr = each(lambda x, y: _dot_tn(x[:, 0:LANES], y[0:C]), aub, bkc)
        g = [_dot_tn(jnp.concatenate([x[:, LANES:], v[i].astype(BF16)], axis=0), y)
             for (i, d), x, y in zip(ch, aub, bkc)]
        for (i, d), x, q, l, gx, t in zip(ch, rhat, qy, lr, g, ltot):
            q_s[d, cid[i]] = (x + q[:, 0:LANES]).astype(BF16)
            lr_s[d, cid[i]] = jnp.where(same_head, l, 0.0).astype(BF16)
            g_s[d, cid[i]] = jnp.where(same_head, gx, 0.0)
            gam_s[d, cid[i]] = jnp.broadcast_to(jnp.exp(t), (8, LANES))
        for i, f0 in enumerate(flat):
            y = bonus[i]
            for d in range(2):
                y = y + xyv[2 * i + d][C:] + qy[2 * i + d][:, LANES:]
            ybuf[pl.ds(f0, C), :] = y

    per_seq = unroll // nseq

    def group(i):
        return [(s, i * per_seq + u) for s in range(nseq) for u in range(per_seq)]

    def prep_body(i, carry):
        prep_group(group(i))
        return carry

    lax.fori_loop(0, n_chunks // per_seq, prep_body, 0)

    chains = [(s, d) for s in range(nseq) for d in range(2)]

    def step_body(i, carry):
        cs = (i, n_chunks - 1 - i)
        cid = [s * n_chunks + cs[d] for s, d in chains]
        sb = bf(carry)
        ys = [_dot_nt(q_s[d, c], x) for (s, d), c, x in zip(chains, cid, sb)]
        ls = [_dot(x, lr_s[d, c]) for (s, d), c, x in zip(chains, cid, sb)]
        for (s, d), y in zip(chains, ys):
            f0 = pl.multiple_of(s * seq + cs[d] * C, C)
            ybuf[pl.ds(f0, C), :] += y
        return tuple(x * gam_s[d, c][0:1] + l + g_s[d, c] for (s, d), c, x, l in zip(chains, cid, carry, ls))

    zero_h = jnp.zeros((HEAD_SIZE, HEAD_SIZE), F32)

    def pair_state(s2):
        return jnp.concatenate([jnp.concatenate([s2[0], zero_h], axis=1),
                                jnp.concatenate([zero_h, s2[1]], axis=1)], axis=0)

    if has_s0:
        init = tuple(pair_state(s0_ref[s, d]) for s, d in chains)
    else:
        init = (jnp.zeros((LANES, LANES), F32),) * len(chains)
    fin = lax.fori_loop(0, n_chunks, step_body, init)
    if want_fin:
        for (s, d), x in zip(chains, fin):
            sfin_ref[s, d, 0] = x[0:HEAD_SIZE, 0:HEAD_SIZE]
            sfin_ref[s, d, 1] = x[HEAD_SIZE:, HEAD_SIZE:]

    def out_body(i, carry):
        grp = group(i)
        rows = [pl.multiple_of(c * C, C) for s, c in grp]
        flat = [pl.multiple_of(s * seq + c * C, C) for s, c in grp]
        y = [ybuf[pl.ds(f0, C), :] for f0 in flat]
        g = [_dot(loa_s[pl.ds(f0, C), LANES:LORA_OUT], wgt_ref[0]) for f0 in flat]
        mean = each(lambda x: seg_sum(x) * (1.0 / HEAD_SIZE), y)
        dlt = each(lambda x, m: x - m, y, mean)
        var = each(lambda x: seg_sum(x * x) * (1.0 / HEAD_SIZE), dlt)
        for (s, c), r0, dx, vx, gx in zip(grp, rows, dlt, var, g):
            yn = dx * lax.rsqrt(vx + GN_EPS) * vec[_V_LNW:_V_LNW + 1] + vec[_V_LNB:_V_LNB + 1]
            out_ref[s, pl.ds(r0, C), :] = (yn * gx).astype(BF16)
        return carry

    lax.fori_loop(0, n_chunks // per_seq, out_body, 0)


def _wkv(proj3, mu3, mu_lora, ww, wa, wgt, vec, s0, grid_mode, want_fin):
    nb, seq, _ = proj3.shape
    n_chunks = seq // CHUNK
    has_s0 = s0 is not None
    group_chunks = 8
    nseq = 2 if (2 * n_chunks <= group_chunks and nb % 2 == 0) else 1
    kern = functools.partial(_wkv_kernel, seq=seq, nseq=nseq, unroll=min(group_chunks, nseq * n_chunks),
                             grid_mode=grid_mode, has_s0=has_s0, want_fin=want_fin)
    state_spec = pl.BlockSpec((nseq, 2, 2, HEAD_SIZE, HEAD_SIZE), lambda b, j: (b, 0, j, 0, 0))
    in_specs = [pl.BlockSpec((nseq, seq, LANES), lambda b, j: (b, 0, R_BLK + j)),
                pl.BlockSpec((nseq, seq, LANES), lambda b, j: (b, 0, K_BLK + j)),
                pl.BlockSpec((nseq, seq, LANES), lambda b, j: (b, 0, V_BLK + j)),
                pl.BlockSpec((nseq, seq, LORA_OUT), lambda b, j: (b, 0, 0)),
                pl.BlockSpec((1, 3, LANES), lambda b, j: (j, 0, 0)),
                pl.BlockSpec((1, LORA_OUT), lambda b, j: (0, 0)),
                pl.BlockSpec((1, LANES, 2 * LANES), lambda b, j: (j, 0, 0)),
                pl.BlockSpec((1, LANES, 2 * LANES), lambda b, j: (j, 0, 0)),
                pl.BlockSpec((1, 2 * LANES, LANES), lambda b, j: (j, 0, 0)),
                pl.BlockSpec((1, 16, LANES), lambda b, j: (j, 0, 0))]
    args = [proj3, proj3, proj3, proj3, mu3, mu_lora, ww, wa, wgt, vec]
    out_specs = [pl.BlockSpec((nseq, seq, LANES), lambda b, j: (b, 0, j))]
    out_shape = [jax.ShapeDtypeStruct((nb, seq, RWKV_WIDTH), BF16)]
    if has_s0:
        in_specs.append(state_spec)
        args.append(s0)
    if want_fin:
        out_specs.append(state_spec)
        out_shape.append(jax.ShapeDtypeStruct((nb, 2, RWKV_HEADS, HEAD_SIZE, HEAD_SIZE), F32))
    slots = nseq * n_chunks
    res = pl.pallas_call(
        kern,
        grid=(nb // nseq, PAIRS),
        in_specs=in_specs,
        out_specs=out_specs,
        out_shape=out_shape,
        scratch_shapes=[pltpu.VMEM((2, slots, CHUNK, LANES), BF16),
                        pltpu.VMEM((2, slots, LANES, LANES), BF16),
                        pltpu.VMEM((2, slots, LANES, LANES), F32),
                        pltpu.VMEM((2, slots, 8, LANES), F32),
                        pltpu.VMEM((nseq * seq, LANES), F32),
                        pltpu.VMEM((nseq * seq, LORA_OUT), BF16)],
        compiler_params=_cparams(("arbitrary", "arbitrary")),
        name="wkv",
    )(*args)
    return res[0], (res[1] if want_fin else None)


def _out_proj_kernel(lru_ref, rw_ref, w1_ref, w2_ref, x_ref, mod_ref, g_ref, x1_ref, h2_ref):
    out = _dot(lru_ref[...], w1_ref[...]) + _dot(rw_ref[...], w2_ref[...])
    m = mod_ref[0]
    g = g_ref[...]
    x1 = x_ref[...] + m[2:3] * _rms(out, g[0:1])
    x1_ref[...] = x1
    h2_ref[...] = (_rms(x1, g[1:2]) * (1.0 + m[4:5]) + m[3:4]).astype(BF16)


def _out_proj(lru_o, rw_o, w_out_b, x2d, mod3, g2, mod_idx, tm=512):
    m = x2d.shape[0]
    half = D_MODEL // 2
    return pl.pallas_call(
        _out_proj_kernel,
        grid=(m // tm,),
        in_specs=[pl.BlockSpec((tm, half), lambda i: (i, 0)),
                  pl.BlockSpec((tm, half), lambda i: (i, 0)),
                  pl.BlockSpec((half, D_MODEL), lambda i: (0, 0)),
                  pl.BlockSpec((half, D_MODEL), lambda i: (1, 0)),
                  pl.BlockSpec((tm, D_MODEL), lambda i: (i, 0)),
                  pl.BlockSpec((1, 6, D_MODEL), lambda i: (mod_idx(i * tm), 0, 0)),
                  pl.BlockSpec((2, D_MODEL), lambda i: (0, 0))],
        out_specs=[pl.BlockSpec((tm, D_MODEL), lambda i: (i, 0)),
                   pl.BlockSpec((tm, D_MODEL), lambda i: (i, 0))],
        out_shape=[jax.ShapeDtypeStruct((m, D_MODEL), F32),
                   jax.ShapeDtypeStruct((m, D_MODEL), BF16)],
        compiler_params=_cparams(("parallel",)),
        name="out_proj",
    )(lru_o, rw_o, w_out_b, w_out_b, x2d, mod3, g2)


def _ffn_kernel(h_ref, wg_ref, wu_ref, wd_ref, x1_ref, mod_ref, g_ref, o_ref, acc):
    f = pl.program_id(1)
    h = h_ref[...]
    gate = _dot(h, wg_ref[...])
    up = _dot(h, wu_ref[...])
    act = (gate * jax.nn.sigmoid(gate) * up).astype(BF16)
    part = _dot(act, wd_ref[...])

    @pl.when(f == 0)
    def _():
        acc[...] = part

    @pl.when(f > 0)
    def _():
        acc[...] += part

    @pl.when(f == pl.num_programs(1) - 1)
    def _():
        o_ref[...] = x1_ref[...] + mod_ref[0][5:6] * _rms(acc[...], g_ref[...])


def _ffn(h2, w_gu_b, w_down_b, x1, mod3, g, mod_idx, tm=512, tf=512):
    m = h2.shape[0]
    nf = FFN_HIDDEN // tf
    return pl.pallas_call(
        _ffn_kernel,
        grid=(m // tm, nf),
        in_specs=[pl.BlockSpec((tm, D_MODEL), lambda i, f: (i, 0)),
                  pl.BlockSpec((D_MODEL, tf), lambda i, f: (0, f)),
                  pl.BlockSpec((D_MODEL, tf), lambda i, f: (0, nf + f)),
                  pl.BlockSpec((tf, D_MODEL), lambda i, f: (f, 0)),
                  pl.BlockSpec((tm, D_MODEL), lambda i, f: (i, 0)),
                  pl.BlockSpec((1, 6, D_MODEL), lambda i, f: (mod_idx(i * tm), 0, 0)),
                  pl.BlockSpec((1, D_MODEL), lambda i, f: (0, 0))],
        out_specs=pl.BlockSpec((tm, D_MODEL), lambda i, f: (i, 0)),
        out_shape=jax.ShapeDtypeStruct((m, D_MODEL), F32),
        scratch_shapes=[pltpu.VMEM((tm, D_MODEL), F32)],
        compiler_params=_cparams(("parallel", "arbitrary")),
        name="ffn",
    )(h2, w_gu_b, w_gu_b, w_down_b, x1, mod3, g)


def _group(x3, mod3, mod_idx, grid_mode, lru_h0, wkv_s0, want_fin, P):
    nb, seq, _ = x3.shape
    x2d = x3.reshape(nb * seq, D_MODEL)
    proj = _in_proj(x2d, mod3, P["norm_mix_pre"], P["w_in"], mod_idx).reshape(nb, seq, PROJ_PAD)
    lru_o, lru_fin = _lru(proj, P["conv_w"], P["conv_b"], P["lru_wg"], P["lru_bg"], P["lru_lam"], lru_h0)
    rw_o, s_fin = _wkv(proj, P["mu3"], P["mu_lora"], P["ww"], P["wa"], P["wgt"], P["vec"], wkv_s0,
                       grid_mode, want_fin)
    x1, h2 = _out_proj(lru_o.reshape(nb * seq, LRU_WIDTH), rw_o.reshape(nb * seq, RWKV_WIDTH),
                       P["w_out"], x2d, mod3, P["g_post_pre"], mod_idx)
    y = _ffn(h2, P["w_gu"], P["w_down"], x1, mod3, P["norm_ffn_post"], mod_idx)
    return y.reshape(nb, seq, D_MODEL), lru_fin, s_fin


def _prep_params(norm_mix_pre, norm_mix_post, norm_ffn_pre, norm_ffn_post, w_in,
                 lru_conv_w, lru_conv_b, lru_wr, lru_br, lru_wi, lru_bi, lru_lambda,
                 rwkv_mu, rwkv_w0, rwkv_w_up, rwkv_a0, rwkv_a_up, rwkv_g_up, rwkv_k_k, rwkv_k_a, rwkv_r_k,
                 rwkv_ln_w, rwkv_ln_b, w_out, ffn_w_gu, ffn_w_down):
    P = {}
    P["norm_mix_pre"] = norm_mix_pre.reshape(1, D_MODEL)
    P["g_post_pre"] = jnp.stack([norm_mix_post, norm_ffn_pre], axis=0)
    P["norm_ffn_post"] = norm_ffn_post.reshape(1, D_MODEL)
    w_in_b = w_in.astype(BF16)
    n_main = IN_WIDTH - LORA_WIDTH
    P["w_in"] = jnp.concatenate([w_in_b[:, n_main:], jnp.zeros((D_MODEL, LORA_PAD - LORA_WIDTH), BF16),
                                 w_in_b[:, :n_main]], axis=1)
    P["conv_w"] = lru_conv_w
    P["conv_b"] = lru_conv_b.reshape(1, LRU_WIDTH)
    P["lru_wg"] = jnp.concatenate([lru_wr[0], lru_wi[0], lru_wr[1], lru_wi[1]], axis=-1).astype(BF16)
    P["lru_bg"] = jnp.concatenate([lru_br[0], lru_bi[0], lru_br[1], lru_bi[1]], axis=-1).reshape(LRU_HEADS, 1, 512)
    P["lru_lam"] = jnp.swapaxes(lru_lambda.reshape(2, LRU_HEADS, LRU_HEAD_DIM), 0, 1)
    P["mu3"] = jnp.swapaxes(rwkv_mu[:3 * RWKV_WIDTH].reshape(3, PAIRS, LANES), 0, 1)
    P["mu_lora"] = jnp.pad(rwkv_mu[3 * RWKV_WIDTH:], (0, LORA_OUT - LORA_WIDTH)).reshape(1, LORA_OUT)

    def per_pair(w):
        return jnp.swapaxes(w.reshape(w.shape[0], PAIRS, LANES), 0, 1)

    wu = jnp.concatenate([per_pair(rwkv_w_up[0]), per_pair(rwkv_w_up[1])], axis=-1)
    au = jnp.concatenate([per_pair(rwkv_a_up[0]), per_pair(rwkv_a_up[1])], axis=-1)
    P["ww"] = jnp.pad(wu, ((0, 0), (0, LANES - DECAY_LORA), (0, 0))).astype(BF16)
    P["wa"] = jnp.pad(au, ((0, 0), (DECAY_LORA, LANES - DECAY_LORA - AAA_LORA), (0, 0))).astype(BF16)
    P["wgt"] = jnp.pad(per_pair(rwkv_g_up), ((0, 0), (0, 2 * LANES - GATE_LORA), (0, 0))).astype(BF16)
    rows = [rwkv_w0[0], rwkv_w0[1], rwkv_a0[0], rwkv_a0[1], rwkv_k_k, rwkv_k_a,
            rwkv_r_k.reshape(RWKV_WIDTH), rwkv_ln_w, rwkv_ln_b]
    vec = jnp.stack(rows + [jnp.zeros_like(rwkv_k_k)] * (16 - len(rows)), axis=0)
    P["vec"] = per_pair(vec)
    P["w_out"] = w_out.astype(BF16)
    P["w_gu"] = ffn_w_gu.astype(BF16)
    P["w_down"] = ffn_w_down.astype(BF16)
    return P


def kernel(x_prompt, x_sample, state_lru, state_wkv, c, c_ctx, norm_mix_pre, norm_mix_post, norm_ffn_pre,
           norm_ffn_post, w_mod, b_mod, w_in, lru_conv_w, lru_conv_b, lru_wr, lru_br, lru_wi, lru_bi,
           lru_lambda, rwkv_mu, rwkv_w0, rwkv_w_up, rwkv_a0, rwkv_a_up, rwkv_g_up, rwkv_k_k, rwkv_k_a,
           rwkv_r_k, rwkv_ln_w, rwkv_ln_b, w_out, ffn_w_gu, ffn_w_down):
    depth = w_in.shape[0]
    nb_p = x_prompt.shape[0]
    nb_s, seq_s, _ = x_sample.shape
    y_p, y_s = x_prompt, x_sample
    new_lru, new_wkv = [], []
    cvec = jnp.concatenate([c_ctx[None, :], c, jnp.zeros((8 - 1 - nb_s, D_MODEL), F32)], axis=0)
    for l in range(depth):
        P = _prep_params(norm_mix_pre[l], norm_mix_post[l], norm_ffn_pre[l], norm_ffn_post[l], w_in[l],
                         lru_conv_w[l], lru_conv_b[l], lru_wr[l], lru_br[l], lru_wi[l], lru_bi[l],
                         lru_lambda[l], rwkv_mu[l], rwkv_w0[l], rwkv_w_up[l], rwkv_a0[l], rwkv_a_up[l],
                         rwkv_g_up[l], rwkv_k_k[l], rwkv_k_a[l], rwkv_r_k[l], rwkv_ln_w[l], rwkv_ln_b[l],
                         w_out[l], ffn_w_gu[l], ffn_w_down[l])
        mod3 = _mod(cvec, w_mod[l], b_mod[l].reshape(1, -1)).reshape(8, 6, D_MODEL)
        y_p, lru_ctx, wkv_ctx = _group(
            y_p, mod3, lambda row: 0, False, jnp.zeros((nb_p, 2, LRU_WIDTH), F32), None, True, P)
        new_lru.append(lru_ctx)
        new_wkv.append(wkv_ctx)
        y_s, _, _ = _group(
            y_s, mod3, lambda row: 1 + row // seq_s, True, state_lru[:, l], state_wkv[:, l], False, P)
    return (y_p, y_s, jnp.stack(new_lru, axis=1), jnp.stack(new_wkv, axis=1))
```
